```python
import math
import jax, jax.numpy as jnp
from jax import lax
import numpy as np

D_MODEL = 4096
BATCH = 4
SEQ = 2048
DEPTH = 2
DEC_BATCH = 8
DEC_SEQ = 1
PAST_LEN = 16384
PAGE_SIZE = 128

N_A_LAYERS = DEPTH // 2
N_B_LAYERS = DEPTH - N_A_LAYERS

SSM_INNER = 3 * D_MODEL // 4
SSM_HEAD_DIM = 64
SSM_HEADS = SSM_INNER // SSM_HEAD_DIM
SSM_GROUPS = 8
SSM_HPG = SSM_HEADS // SSM_GROUPS
SSM_STATE = 128
SSM_CONV = 4
SSM_CHUNK = 128
SSM_XBC = SSM_INNER + 2 * SSM_GROUPS * SSM_STATE

DIL_PATTERNS = ((128, 1), (512, 4), (2048, 16))
N_DIL = len(DIL_PATTERNS)
DIL_HEADS = 8
HEAD_DIM = 128
ROT_DIM = HEAD_DIM // 4
ROPE_THETA = 500000.0
DIL_Q = N_DIL * DIL_HEADS * HEAD_DIM
DIL_OUT = DIL_HEADS * HEAD_DIM
KV_COLS = N_DIL * 2 * DIL_HEADS * HEAD_DIM
Q_BLOCK = 128

N_MEM = 256
MEM_HEADS = 4
MEM_HEAD_DIM = D_MODEL // 16
MEM_W = MEM_HEADS * MEM_HEAD_DIM

D_FF = 256 * ((8 * D_MODEL // 3 + 255) // 256)
FFN_CONV = 3

EPS = 1e-6
A_IN = SSM_INNER + SSM_XBC + SSM_HEADS + MEM_W
A_SPLITS = [SSM_INNER, SSM_INNER + SSM_XBC, SSM_INNER + SSM_XBC + SSM_HEADS]
B_IN = DIL_Q + MEM_W
A_MIX = SSM_INNER + MEM_W
B_MIX = DIL_OUT + MEM_W

kernel_name = 'yoco_mamba2_dilated_window_hybrid_step'


def rmsnorm(x, g):
    xf = x.astype(jnp.float32)
    y = xf * lax.rsqrt(jnp.mean(xf * xf, axis=-1, keepdims=True) + EPS)
    return (y * g.astype(jnp.float32)).astype(x.dtype)


def rope_partial(x, pos):
    half = ROT_DIM // 2
    inv_freq = jnp.exp(-(2.0 * jnp.arange(half, dtype=jnp.float32) / ROT_DIM) * math.log(ROPE_THETA))
    ang = pos.astype(jnp.float32)[:, None] * inv_freq[None, :]
    shape = (1, pos.shape[0]) + (1,) * (x.ndim - 3) + (half,)
    cos = jnp.cos(ang).reshape(shape)
    sin = jnp.sin(ang).reshape(shape)
    xr = x[..., :ROT_DIM].astype(jnp.float32)
    x1, x2 = xr[..., :half], xr[..., half:]
    rot = jnp.concatenate([x1 * cos - x2 * sin, x2 * cos + x1 * sin], axis=-1).astype(x.dtype)
    return jnp.concatenate([rot, x[..., ROT_DIM:]], axis=-1)


def causal_dwconv(x, prev, w, b):
    width = w.shape[0]
    t_len = x.shape[1]
    xp = jnp.concatenate([prev.astype(x.dtype), x], axis=1)
    y = b + sum(xp[:, k:k + t_len] * w[k] for k in range(width))
    return y, xp[:, xp.shape[1] - (width - 1):]


def ssd_scan(xdt_in, dt, a_neg, bm, cm, h0):
    x = xdt_in
    bsz, t_len = x.shape[:2]
    c = min(SSM_CHUNK, t_len)
    n_c = -(-t_len // c)
    pad = n_c * c - t_len

    def chunks(u):
        u = jnp.pad(u, ((0, 0), (0, pad)) + ((0, 0),) * (u.ndim - 2))
        return jnp.moveaxis(u.reshape((bsz, n_c, c) + u.shape[2:]), 1, 0)

    xdt = x * dt[..., None]
    log_a = dt * a_neg
    tri = jnp.tril(jnp.ones((c, c), dtype=bool))

    def step(h, inp):
        xdt_c, la_c, b_c, c_c = inp
        cum = jnp.cumsum(la_c, axis=1)
        seg = cum[:, :, None] - cum[:, None, :]
        decay = jnp.exp(jnp.where(tri[None, :, :, None, None], seg, -jnp.inf))
        cb = jnp.einsum('btgn,bsgn->btsg', c_c, b_c)
        y = jnp.einsum('btsgh,bsghp->btghp', cb[..., None] * decay, xdt_c)
        y = y + jnp.einsum('btgn,bghpn->btghp', c_c, h) * jnp.exp(cum)[..., None]
        w_end = jnp.exp(cum[:, -1:] - cum)
        h = h * jnp.exp(cum[:, -1])[..., None, None] + jnp.einsum('bsgh,bsghp,bsgn->bghpn', w_end, xdt_c, b_c)
        return h, y

    h_last, ys = lax.scan(step, h0, (chunks(xdt), chunks(log_a), chunks(bm), chunks(cm)))
    y = jnp.moveaxis(ys, 0, 1).reshape((bsz, n_c * c) + x.shape[2:])[:, :t_len]
    return y, h_last


def mamba2_mixer(z, xbc, dt_raw, h_prev, conv_prev, w_conv, b_conv, dt_bias, a_log, d_skip, g_out):
    bsz, t_len, _ = z.shape
    xbc, conv_new = causal_dwconv(xbc, conv_prev, w_conv, b_conv)
    xbc = jax.nn.silu(xbc.astype(jnp.float32))
    xs, bm, cm = jnp.split(xbc, [SSM_INNER, SSM_INNER + SSM_GROUPS * SSM_STATE], axis=-1)
    xs = xs.reshape(bsz, t_len, SSM_GROUPS, SSM_HPG, SSM_HEAD_DIM)
    bm = bm.reshape(bsz, t_len, SSM_GROUPS, SSM_STATE)
    cm = cm.reshape(bsz, t_len, SSM_GROUPS, SSM_STATE)
    dt = jax.nn.softplus(dt_raw.astype(jnp.float32) + dt_bias.astype(jnp.float32))
    dt = dt.reshape(bsz, t_len, SSM_GROUPS, SSM_HPG)
    a_neg = -jnp.exp(a_log.astype(jnp.float32)).reshape(SSM_GROUPS, SSM_HPG)
    h0 = h_prev.astype(jnp.float32).reshape(bsz, SSM_GROUPS, SSM_HPG, SSM_HEAD_DIM, SSM_STATE)
    y, h_last = ssd_scan(xs, dt, a_neg, bm, cm, h0)
    y = y + d_skip.astype(jnp.float32).reshape(SSM_GROUPS, SSM_HPG)[..., None] * xs
    u = (y.reshape(bsz, t_len, SSM_INNER) * jax.nn.silu(z.astype(jnp.float32)))
    u = u.reshape(bsz, t_len, SSM_GROUPS, SSM_INNER // SSM_GROUPS)
    u = u * lax.rsqrt(jnp.mean(u * u, axis=-1, keepdims=True) + EPS)
    u = u.reshape(bsz, t_len, SSM_INNER) * g_out.astype(jnp.float32)
    h_out = h_last.reshape(bsz, SSM_HEADS, SSM_HEAD_DIM, SSM_STATE).astype(h_prev.dtype)
    return u.astype(z.dtype), h_out, conv_new


def memory_kv(mem, g_norm, w_kv, g_k):
    bsz, n, _ = mem.shape
    kv = (rmsnorm(mem, g_norm) @ w_kv).reshape(bsz, n, 2, MEM_HEADS, MEM_HEAD_DIM)
    k = rmsnorm(kv[:, :, 0], g_k)
    return jnp.stack([k, kv[:, :, 1]], axis=2)


def memory_attention(q, mem_kv):
    s = jnp.einsum('bthd,bmhd->bhtm', q.astype(jnp.float32), mem_kv[:, :, 0].astype(jnp.float32)) * (MEM_HEAD_DIM ** -0.5)
    p = jax.nn.softmax(s, axis=-1)
    o = jnp.einsum('bhtm,bmhd->bthd', p, mem_kv[:, :, 1].astype(jnp.float32))
    return o.astype(q.dtype)


def shared_window_kv(x, pos, g_norm, w_kv, g_k):
    bsz, t_len, _ = x.shape
    kv = (rmsnorm(x, g_norm) @ w_kv).reshape(bsz, t_len, N_DIL, 2, DIL_HEADS, HEAD_DIM)
    k = rope_partial(rmsnorm(kv[:, :, :, 0], g_k[:, None, :]), pos)
    return jnp.stack([k, kv[:, :, :, 1]], axis=3)


def dilated_attention(q, kv_full, past_lens):
    bsz, t_len = q.shape[:2]
    bq = min(Q_BLOCK, t_len)
    n_blk = -(-t_len // bq)
    pad = n_blk * bq - t_len
    qp = jnp.pad(q, ((0, 0), (0, pad), (0, 0), (0, 0), (0, 0)))
    kvp = [jnp.pad(kv, ((0, 0), (w, pad), (0, 0), (0, 0), (0, 0))) for kv, (w, _) in zip(kv_full, DIL_PATTERNS)]
    scale = HEAD_DIM ** -0.5
    i_idx = np.arange(bq)[:, None]

    def block(q0):
        qb = lax.dynamic_slice_in_dim(qp, q0, bq, axis=1).astype(jnp.float32)
        outs, lses = [], []
        for g, (w, r) in enumerate(DIL_PATTERNS):
            n_k = w // r + 1
            k_idx = np.arange(n_k)[None, :]
            sl = lax.dynamic_slice_in_dim(kvp[g], q0 + past_lens[g], bq + w, axis=1)
            kvg = sl[:, i_idx + w - k_idx * r].astype(jnp.float32)
            valid = (q0 + past_lens[g] + i_idx - k_idx * r) >= 0
            s = jnp.einsum('bqhd,bqkhd->bhqk', qb[:, :, g], kvg[:, :, :, 0]) * scale
            s = jnp.where(valid[None, None], s, -jnp.inf)
            m = jnp.max(s, axis=-1, keepdims=True)
            p = jnp.exp(s - m)
            den = jnp.sum(p, axis=-1, keepdims=True)
            o = jnp.einsum('bhqk,bqkhd->bqhd', p / den, kvg[:, :, :, 1])
            outs.append(o)
            lses.append((m + jnp.log(den))[..., 0])
        wts = jax.nn.softmax(jnp.stack(lses, axis=0), axis=0)
        wts = jnp.swapaxes(wts, 2, 3)[..., None]
        return jnp.sum(wts * jnp.stack(outs, axis=0), axis=0).astype(q.dtype)

    ob = lax.map(block, jnp.arange(n_blk, dtype=jnp.int32) * bq)
    o = jnp.moveaxis(ob, 0, 1).reshape(bsz, n_blk * bq, DIL_HEADS, HEAD_DIM)
    return o[:, :t_len]


def conv_ffn(x, prev, g, w_up, w_conv, b_conv, w_down):
    gate, up = jnp.split(rmsnorm(x, g) @ w_up, [D_FF], axis=-1)
    gate, new_prev = causal_dwconv(gate, prev, w_conv, b_conv)
    return (jax.nn.silu(gate) * up) @ w_down, new_prev


def trunk(x, pos, mem_kv, ssm_prev, ssm_conv_prev, ffn_prev, win_past, p):
    bsz, t_len, _ = x.shape
    ssm_out, ssm_conv_out, ffn_out = [], [], []
    kv_new, kv_full = None, None
    past_lens = [wp.shape[1] for wp in win_past]
    for i in range(DEPTH):
        h = rmsnorm(x, p['g_mix'][i])
        if i < N_A_LAYERS:
            a = i
            z, xbc, dt_raw, q_mem = jnp.split(h @ p['w_in_a'][a], A_SPLITS, axis=-1)
            y_mix, s_new, c_new = mamba2_mixer(z, xbc, dt_raw, ssm_prev[a], ssm_conv_prev[a], p['w_conv_a'][a], p['b_conv_a'][a], p['dt_bias_a'][a], p['a_log_a'][a], p['d_skip_a'][a], p['g_ssm_out_a'][a])
            ssm_out.append(s_new)
            ssm_conv_out.append(c_new)
            w_out = p['w_out_a'][a]
        else:
            b = i - N_A_LAYERS
            if i == N_A_LAYERS:
                kv_new = shared_window_kv(x, pos, p['g_kv'], p['w_kv'], p['g_k_dil'])
                kv_full = [jnp.concatenate([win_past[g].astype(x.dtype), kv_new[:, :, g]], axis=1) for g in range(N_DIL)]
            q_dil, q_mem = jnp.split(h @ p['w_in_b'][b], [DIL_Q], axis=-1)
            q = rmsnorm(q_dil.reshape(bsz, t_len, N_DIL, DIL_HEADS, HEAD_DIM), p['g_q_dil'][b][:, None, :])
            q = rope_partial(q, pos)
            y_mix = dilated_attention(q, kv_full, past_lens).reshape(bsz, t_len, DIL_OUT)
            w_out = p['w_out_b'][b]
        qm = rmsnorm(q_mem.reshape(bsz, t_len, MEM_HEADS, MEM_HEAD_DIM), p['g_mem_q'][i])
        y_mem = memory_attention(qm, mem_kv[i]).reshape(bsz, t_len, MEM_W)
        x = x + jnp.concatenate([y_mix.astype(x.dtype), y_mem.astype(x.dtype)], axis=-1) @ w_out
        f, f_new = conv_ffn(x, ffn_prev[i], p['g_ffn'][i], p['w_ffn_up'][i], p['w_ffn_conv'][i], p['b_ffn_conv'][i], p['w_ffn_down'][i])
        ffn_out.append(f_new)
        x = x + f
    return x, jnp.stack(ssm_out, axis=0), jnp.stack(ssm_conv_out, axis=0), jnp.stack(ffn_out, axis=0), kv_new


def setup_inputs(seed: int = 0) -> dict:
    key = jax.random.key(seed)
    ks = iter(jax.random.split(key, 48))
    f32 = jnp.float32

    def nrm(shape, scale=1.0):
        return jax.random.normal(next(ks), shape, f32) * scale

    def gain(shape):
        return 1.0 + nrm(shape, 0.02)

    n_a, n_b = N_A_LAYERS, N_B_LAYERS
    l_win = [min(w, PAST_LEN) for w, _ in DIL_PATTERNS]
    inp = {}
    inp['x_prompt'] = nrm((BATCH, SEQ, D_MODEL))
    inp['x_sample'] = nrm((DEC_BATCH, DEC_SEQ, D_MODEL))
    inp['state_ssm'] = nrm((n_a, DEC_BATCH, SSM_HEADS, SSM_HEAD_DIM, SSM_STATE), 0.5)
    inp['state_ssm_conv'] = nrm((n_a, DEC_BATCH, SSM_CONV - 1, SSM_XBC))
    inp['state_ffn_conv'] = nrm((DEPTH, DEC_BATCH, FFN_CONV - 1, D_FF))
    inp['cache_mem_kv'] = nrm((DEPTH, DEC_BATCH, N_MEM, 2, MEM_HEADS, MEM_HEAD_DIM))
    inp['cache_win_kv0'] = nrm((DEC_BATCH, l_win[0], 2, DIL_HEADS, HEAD_DIM))
    inp['cache_win_kv1'] = nrm((DEC_BATCH, l_win[1], 2, DIL_HEADS, HEAD_DIM))
    inp['cache_win_kv2'] = nrm((DEC_BATCH, l_win[2], 2, DIL_HEADS, HEAD_DIM))
    inp['mem_prompt'] = nrm((BATCH, N_MEM, D_MODEL))
    inp['g_mix'] = gain((DEPTH, D_MODEL))
    inp['w_in_a'] = nrm((n_a, D_MODEL, A_IN), D_MODEL ** -0.5)
    inp['w_conv_a'] = nrm((n_a, SSM_CONV, SSM_XBC), SSM_CONV ** -0.5)
    inp['b_conv_a'] = nrm((n_a, SSM_XBC), 0.02)
    dt0 = jnp.exp(jax.random.uniform(next(ks), (n_a, SSM_HEADS), f32, math.log(1e-3), math.log(1e-1)))
    inp['dt_bias_a'] = dt0 + jnp.log(-jnp.expm1(-dt0))
    inp['a_log_a'] = jnp.log(jax.random.uniform(next(ks), (n_a, SSM_HEADS), f32, 1.0, 16.0))
    inp['d_skip_a'] = 1.0 + nrm((n_a, SSM_HEADS), 0.1)
    inp['g_ssm_out_a'] = gain((n_a, SSM_INNER))
    inp['w_out_a'] = nrm((n_a, A_MIX, D_MODEL), A_MIX ** -0.5)
    inp['g_kv'] = gain((D_MODEL,))
    inp['w_kv'] = nrm((D_MODEL, KV_COLS), D_MODEL ** -0.5)
    inp['g_k_dil'] = gain((N_DIL, HEAD_DIM))
    inp['w_in_b'] = nrm((n_b, D_MODEL, B_IN), D_MODEL ** -0.5)
    inp['g_q_dil'] = gain((n_b, N_DIL, HEAD_DIM))
    inp['w_out_b'] = nrm((n_b, B_MIX, D_MODEL), B_MIX ** -0.5)
    inp['g_mem'] = gain((DEPTH, D_MODEL))
    inp['w_mem_kv'] = nrm((DEPTH, D_MODEL, 2 * MEM_W), D_MODEL ** -0.5)
    inp['g_mem_q'] = gain((DEPTH, MEM_HEAD_DIM))
    inp['g_mem_k'] = gain((DEPTH, MEM_HEAD_DIM))
    inp['g_ffn'] = gain((DEPTH, D_MODEL))
    inp['w_ffn_up'] = nrm((DEPTH, D_MODEL, 2 * D_FF), D_MODEL ** -0.5)
    inp['w_ffn_conv'] = nrm((DEPTH, FFN_CONV, D_FF), FFN_CONV ** -0.5)
    inp['b_ffn_conv'] = nrm((DEPTH, D_FF), 0.02)
    inp['w_ffn_down'] = nrm((DEPTH, D_FF, D_MODEL), D_FF ** -0.5)
    return inp


def reference(x_prompt, x_sample, state_ssm, state_ssm_conv, state_ffn_conv, cache_mem_kv, cache_win_kv0, cache_win_kv1, cache_win_kv2, mem_prompt, g_mix, w_in_a, w_conv_a, b_conv_a, dt_bias_a, a_log_a, d_skip_a, g_ssm_out_a, w_out_a, g_kv, w_kv, g_k_dil, w_in_b, g_q_dil, w_out_b, g_mem, w_mem_kv, g_mem_q, g_mem_k, g_ffn, w_ffn_up, w_ffn_conv, b_ffn_conv, w_ffn_down):
    p = dict(g_mix=g_mix, w_in_a=w_in_a, w_conv_a=w_conv_a, b_conv_a=b_conv_a, dt_bias_a=dt_bias_a, a_log_a=a_log_a, d_skip_a=d_skip_a, g_ssm_out_a=g_ssm_out_a, w_out_a=w_out_a, g_kv=g_kv, w_kv=w_kv, g_k_dil=g_k_dil, w_in_b=w_in_b, g_q_dil=g_q_dil, w_out_b=w_out_b, g_mem_q=g_mem_q, g_ffn=g_ffn, w_ffn_up=w_ffn_up, w_ffn_conv=w_ffn_conv, b_ffn_conv=b_ffn_conv, w_ffn_down=w_ffn_down)
    bp, sp = x_prompt.shape[0], x_prompt.shape[1]
    ds = x_sample.shape[1]
    dt_p = x_prompt.dtype
    mem_kv_p = jnp.stack([memory_kv(mem_prompt, g_mem[i], w_mem_kv[i], g_mem_k[i]) for i in range(DEPTH)], axis=0)
    ssm0 = jnp.zeros((N_A_LAYERS, bp, SSM_HEADS, SSM_HEAD_DIM, SSM_STATE), dt_p)
    conv0 = jnp.zeros((N_A_LAYERS, bp, SSM_CONV - 1, SSM_XBC), dt_p)
    ffn0 = jnp.zeros((DEPTH, bp, FFN_CONV - 1, D_FF), dt_p)
    win0 = [jnp.zeros((bp, 0, 2, DIL_HEADS, HEAD_DIM), dt_p) for _ in range(N_DIL)]
    y_p, ssm_p, conv_p, ffn_p, kv_p = trunk(x_prompt, jnp.arange(sp, dtype=jnp.int32), mem_kv_p, ssm0, conv0, ffn0, win0, p)
    pos_s = PAST_LEN + jnp.arange(ds, dtype=jnp.int32)
    y_s, ssm_s, conv_s, ffn_s, kv_s = trunk(x_sample, pos_s, cache_mem_kv, state_ssm, state_ssm_conv, state_ffn_conv, [cache_win_kv0, cache_win_kv1, cache_win_kv2], p)
    l0 = min(DIL_PATTERNS[0][0], sp)
    l1 = min(DIL_PATTERNS[1][0], sp)
    l2 = min(DIL_PATTERNS[2][0], sp)
    return (y_p, y_s, ssm_p, ssm_s, conv_p, conv_s, ffn_p, ffn_s, mem_kv_p, kv_p[:, sp - l0:, 0], kv_p[:, sp - l1:, 1], kv_p[:, sp - l2:, 2], kv_s[:, :, 0], kv_s[:, :, 1], kv_s[:, :, 2])
```

```python
import functools
import math

import jax
import jax.numpy as jnp
from jax import lax
from jax.experimental import pallas as pl
from jax.experimental.pallas import tpu as pltpu

F32 = jnp.float32
BF16 = jnp.bfloat16
EPS = 1e-6
LANES = 128
V7X_VMEM_BYTES = 64 * 2**20

SSM_HEAD_DIM = 64
SSM_HEADS = 48
SSM_GROUPS = 8
SSM_STATE = 128
SSM_INNER = SSM_HEADS * SSM_HEAD_DIM
SSM_BC = SSM_GROUPS * SSM_STATE
SSM_XBC = SSM_INNER + 2 * SSM_BC
SSM_CONV = 4
SSM_CHUNK = 128
SSM_PAIRS = SSM_HEADS // 2
DIL_PATTERNS = ((128, 1), (512, 4), (2048, 16))
DIL_HEADS = 8
HEAD_DIM = 128
ROT_DIM = HEAD_DIM // 4
ROPE_THETA = 500000.0
N_MEM = 256
MEM_HEADS = 4
MEM_HEAD_DIM = 256
MEM_W = MEM_HEADS * MEM_HEAD_DIM
FFN_CONV = 3
PAST_LEN = 16384


def _cparams(sem, vmem_bytes):
    limit = min(int(vmem_bytes * 1.25) + (4 << 20), V7X_VMEM_BYTES - (6 << 20))
    return pltpu.CompilerParams(dimension_semantics=sem or None, vmem_limit_bytes=limit)


def _silu(x):
    return x * jax.nn.sigmoid(x)


def _softplus(x):
    return jnp.maximum(x, 0.0) + jnp.log1p(jnp.exp(-jnp.abs(x)))


def _rmsnorm_kernel(x_ref, g_ref, o_ref):
    x = x_ref[...]
    ms = jnp.mean(x * x, axis=-1, keepdims=True)
    o_ref[...] = (x * lax.rsqrt(ms + EPS) * g_ref[...]).astype(o_ref.dtype)


def rmsnorm_rows(x, g, tm=256):
    m, d = x.shape
    tm = min(tm, m)
    return pl.pallas_call(
        _rmsnorm_kernel,
        grid=(m // tm,),
        in_specs=[pl.BlockSpec((tm, d), lambda i: (i, 0)),
                  pl.BlockSpec((1, d), lambda i: (0, 0))],
        out_specs=pl.BlockSpec((tm, d), lambda i: (i, 0)),
        out_shape=jax.ShapeDtypeStruct((m, d), BF16),
        compiler_params=_cparams(("parallel",), 2 * tm * d * 6),
        name="rmsnorm_rows",
    )(x, g.reshape(1, d).astype(F32))


def rope_tables(pos):
    half = ROT_DIM // 2
    inv_freq = jnp.exp(-(2.0 * jnp.arange(half, dtype=F32) / ROT_DIM) * math.log(ROPE_THETA))
    ang = pos.astype(F32)[:, None] * inv_freq[None, :]
    cos, sin = jnp.cos(ang), jnp.sin(ang)
    n = pos.shape[0]
    ones = jnp.ones((n, HEAD_DIM - ROT_DIM), F32)
    zeros = jnp.zeros((n, HEAD_DIM - ROT_DIM), F32)
    zh = jnp.zeros((n, half), F32)
    c = jnp.concatenate([cos, cos, ones], axis=1)
    s_lo = jnp.concatenate([-sin, zh, zeros], axis=1)
    s_hi = jnp.concatenate([zh, sin, zeros], axis=1)
    return c, s_lo, s_hi


def _head_norm(blk, gain, hd):
    ms = jnp.sum(blk * blk, axis=-1, keepdims=True) * (1.0 / hd)
    return blk * lax.rsqrt(ms + EPS) * gain


def _mm_kernel(*refs, nk, k_tail, modes, has_gain, has_rope, has_res, tn):
    it = iter(refs)
    a_ref, w_ref = next(it), next(it)
    gain_ref = next(it) if has_gain else None
    rope_refs = (next(it), next(it), next(it)) if has_rope else None
    res_ref = next(it) if has_res else None
    o_ref, acc_ref = next(it), next(it)
    j, k = pl.program_id(1), pl.program_id(2)

    def accumulate(a, w):
        p = jnp.dot(a.astype(BF16), w.astype(BF16), preferred_element_type=F32)
        if nk == 1:
            acc_ref[...] = p
        else:
            @pl.when(k == 0)
            def _():
                acc_ref[...] = p

            @pl.when(k > 0)
            def _():
                acc_ref[...] += p

    if k_tail:
        @pl.when(k < nk - 1)
        def _():
            accumulate(a_ref[...], w_ref[...])

        @pl.when(k == nk - 1)
        def _():
            a, w = a_ref[...], w_ref[...]
            ca = lax.broadcasted_iota(jnp.int32, a.shape, 1)
            rw = lax.broadcasted_iota(jnp.int32, w.shape, 0)
            accumulate(jnp.where(ca < k_tail, a, jnp.zeros_like(a)),
                       jnp.where(rw < k_tail, w, jnp.zeros_like(w)))
    else:
        accumulate(a_ref[...], w_ref[...])

    def epilogue(mode):
        if mode == "plain":
            acc = acc_ref[...]
            if has_res:
                acc = acc + res_ref[...]
            o_ref[...] = acc.astype(o_ref.dtype)
            return
        hd = 256 if mode == "norm256" else 128
        for h0 in range(0, tn, hd):
            y = _head_norm(acc_ref[:, h0:h0 + hd], gain_ref[:, h0:h0 + hd], hd)
            if mode == "norm128rope":
                c_ref, slo_ref, shi_ref = rope_refs
                half = ROT_DIM // 2
                y = (y * c_ref[...]
                     + pltpu.roll(y, hd - half, 1) * slo_ref[...]
                     + pltpu.roll(y, half, 1) * shi_ref[...])
            o_ref[:, h0:h0 + hd] = y.astype(o_ref.dtype)

    @pl.when(k == nk - 1)
    def _():
        distinct = sorted(set(modes))
        if len(distinct) == 1:
            epilogue(distinct[0])
        else:
            for mode in distinct:
                cond = None
                for jj, mj in enumerate(modes):
                    if mj == mode:
                        c = j == jj
                        cond = c if cond is None else jnp.logical_or(cond, c)
                pl.when(cond)(functools.partial(epilogue, mode))


def matmul(a, w, *, n, tm, tn, tk, w_col0=0, out_dtype=F32, modes=None, gain=None,
           rope=None, res=None, name="matmul"):
    m, kdim = a.shape
    tm, tk = min(tm, m), min(tk, kdim)
    assert m % tm == 0 and n % tn == 0 and w_col0 % tn == 0
    nk = pl.cdiv(kdim, tk)
    k_tail = kdim - (nk - 1) * tk if kdim % tk else 0
    nj = n // tn
    modes = tuple(modes) if modes is not None else ("plain",) * nj
    assert len(modes) == nj
    cb = w_col0 // tn
    in_specs = [pl.BlockSpec((tm, tk), lambda i, j, k: (i, k)),
                pl.BlockSpec((tk, tn), lambda i, j, k: (k, j + cb))]
    args = [a, w]
    vmem = 2 * tm * tk * a.dtype.itemsize + 2 * tk * tn * 4 + tm * tn * 4
    vmem += 2 * tm * tn * jnp.dtype(out_dtype).itemsize + tk * tn * 2 + tm * tn * 4
    if gain is not None:
        in_specs.append(pl.BlockSpec((1, tn), lambda i, j, k: (0, j)))
        args.append(gain.reshape(1, n).astype(F32))
    if rope is not None:
        period = rope[0].shape[0]
        assert period % tm == 0
        nper = period // tm
        for t in rope:
            in_specs.append(pl.BlockSpec((tm, HEAD_DIM), lambda i, j, k: (i % nper, 0)))
            args.append(t)
    if res is not None:
        in_specs.append(pl.BlockSpec((tm, tn), lambda i, j, k: (i, j)))
        args.append(res)
        vmem += 2 * tm * tn * 4
    kern = functools.partial(_mm_kernel, nk=nk, k_tail=k_tail, modes=modes,
                             has_gain=gain is not None, has_rope=rope is not None,
                             has_res=res is not None, tn=tn)
    return pl.pallas_call(
        kern,
        grid=(m // tm, nj, nk),
        in_specs=in_specs,
        out_specs=pl.BlockSpec((tm, tn), lambda i, j, k: (i, j)),
        out_shape=jax.ShapeDtypeStruct((m, n), out_dtype),
        scratch_shapes=[pltpu.VMEM((tm, tn), F32)],
        compiler_params=_cparams(("parallel", "parallel", "arbitrary"), vmem),
        name=name,
    )(*args)


def _mem_attn_kernel(q_ref, kv_ref, o_ref):
    scale = MEM_HEAD_DIM ** -0.5
    for h in range(MEM_HEADS):
        lo = h * MEM_HEAD_DIM
        q = q_ref[0, :, lo:lo + MEM_HEAD_DIM].astype(BF16)
        tq = q.shape[0]
        if tq < 8:
            q = jnp.broadcast_to(q, (8, MEM_HEAD_DIM))
        k = kv_ref[0, :, lo:lo + MEM_HEAD_DIM].astype(BF16)
        v = kv_ref[0, :, MEM_W + lo:MEM_W + lo + MEM_HEAD_DIM].astype(BF16)
        s = lax.dot_general(q, k, (((1,), (1,)), ((), ())), preferred_element_type=F32) * scale
        p = jnp.exp(s - jnp.max(s, axis=-1, keepdims=True))
        den = jnp.sum(p, axis=-1, keepdims=True)
        o = jnp.dot(p.astype(BF16), v, preferred_element_type=F32) / den
        o_ref[0, :, lo:lo + MEM_HEAD_DIM] = o[0:tq].astype(o_ref.dtype)


def mem_attention(q, q_colblk, mem_kv, tq=512):
    b, t, _ = q.shape
    tq = min(tq, t)
    return pl.pallas_call(
        _mem_attn_kernel,
        grid=(b, t // tq),
        in_specs=[pl.BlockSpec((1, tq, MEM_W), lambda i, j: (i, j, q_colblk)),
                  pl.BlockSpec((1, N_MEM, 2 * MEM_W), lambda i, j: (i, 0, 0))],
        out_specs=pl.BlockSpec((1, tq, MEM_W), lambda i, j: (i, j, 0)),
        out_shape=jax.ShapeDtypeStruct((b, t, MEM_W), BF16),
        compiler_params=_cparams(("parallel", "parallel"), 2 * (tq * MEM_W * 6 + N_MEM * 2 * MEM_W * 4) + 8 * tq * N_MEM * 4),
        name="mem_attention",
    )(q, mem_kv)


def _split3_dot(lhs_bf16, x):
    hi = x.astype(BF16)
    r1 = x - hi.astype(F32)
    mid = r1.astype(BF16)
    lo = (r1 - mid.astype(F32)).astype(BF16)
    out = jnp.dot(lhs_bf16, hi, preferred_element_type=F32)
    out += jnp.dot(lhs_bf16, mid, preferred_element_type=F32)
    out += jnp.dot(lhs_bf16, lo, preferred_element_type=F32)
    return out


def _ssd_kernel(z_ref, x_ref, b_ref, c_ref, dt_ref, wconv_ref, bconv_ref, dtb_ref, aneg_ref,
                dskip_ref, gout_ref, u_ref, hout_ref, xp_ref, xs_ref, h_ref, y_ref):
    ck = pl.program_id(1)
    n_ck = pl.num_programs(1)
    c = SSM_CHUNK

    @pl.when(ck == 0)
    def _():
        xp_ref[0:8, :] = jnp.zeros((8, SSM_XBC), F32)
        h_ref[...] = jnp.zeros_like(h_ref)

    xp_ref[8:8 + c, 0:SSM_INNER] = x_ref[...]
    xp_ref[8:8 + c, SSM_INNER:SSM_INNER + SSM_BC] = b_ref[...]
    xp_ref[8:8 + c, SSM_INNER + SSM_BC:SSM_XBC] = c_ref[...]

    slab = 512
    for c0 in range(0, SSM_XBC, slab):
        acc = bconv_ref[:, c0:c0 + slab] + wconv_ref[3:4, c0:c0 + slab] * xp_ref[8:8 + c, c0:c0 + slab]
        for tap in range(SSM_CONV - 1):
            off = 8 - (SSM_CONV - 1 - tap)
            acc = acc + wconv_ref[tap:tap + 1, c0:c0 + slab] * xp_ref[off:off + c, c0:c0 + slab]
        xs_ref[:, c0:c0 + slab] = _silu(acc)
    xp_ref[0:8, :] = xp_ref[c:c + 8, :]

    lane = lax.broadcasted_iota(jnp.int32, (c, LANES), 1)
    row = lax.broadcasted_iota(jnp.int32, (c, LANES), 0)
    dt = jnp.where(lane < SSM_HEADS, _softplus(dt_ref[...] + dtb_ref[...]), 0.0)
    la = dt * aneg_ref[...]
    tri = row >= lane
    cum = _split3_dot(jnp.where(tri, 1.0, 0.0).astype(BF16), la)
    cum_t = cum.T
    ecum = jnp.exp(cum)
    wend = jnp.exp(cum[c - 1:c, :] - cum)
    elast_t = jnp.exp(cum_t[:, c - 1:c])
    left = lane < SSM_HEAD_DIM
    top = row < SSM_HEAD_DIM

    def pair(col0, col1):
        return jnp.where(left, col0, col1)

    for g in range(SSM_GROUPS):
        bg = xs_ref[:, SSM_INNER + g * SSM_STATE:SSM_INNER + (g + 1) * SSM_STATE].astype(BF16)
        cg = xs_ref[:, SSM_INNER + SSM_BC + g * SSM_STATE:SSM_INNER + SSM_BC + (g + 1) * SSM_STATE].astype(BF16)
        cb = lax.dot_general(cg, bg, (((1,), (1,)), ((), ())), preferred_element_type=F32)
        for q in range(SSM_PAIRS // SSM_GROUPS):
            pr = g * (SSM_PAIRS // SSM_GROUPS) + q
            h0, h1 = 2 * pr, 2 * pr + 1
            xpair = xs_ref[:, pr * LANES:(pr + 1) * LANES]
            xdt = xpair * pair(dt[:, h0:h0 + 1], dt[:, h1:h1 + 1])
            xdt_b = xdt.astype(BF16)
            ys = []
            for hh in (h0, h1):
                seg = cum[:, hh:hh + 1] - cum_t[hh:hh + 1, :]
                decay = jnp.exp(jnp.where(tri, seg, -jnp.inf))
                ys.append(jnp.dot((cb * decay).astype(BF16), xdt_b, preferred_element_type=F32))
            y = pair(ys[0], ys[1])
            hp = h_ref[pr]
            y_in = lax.dot_general(cg, hp.astype(BF16), (((1,), (1,)), ((), ())), preferred_element_type=F32)
            y = y + y_in * pair(ecum[:, h0:h0 + 1], ecum[:, h1:h1 + 1])
            xw = xdt * pair(wend[:, h0:h0 + 1], wend[:, h1:h1 + 1])
            upd = jnp.dot(xw.T.astype(BF16), bg, preferred_element_type=F32)
            keep = jnp.where(top, elast_t[h0:h0 + 1, :], elast_t[h1:h1 + 1, :])
            h_ref[pr] = hp * keep + upd
            y_ref[:, pr * LANES:(pr + 1) * LANES] = y + dskip_ref[:, pr * LANES:(pr + 1) * LANES] * xpair

    gw = SSM_INNER // SSM_GROUPS
    for g in range(SSM_GROUPS):
        sl = slice(g * gw, (g + 1) * gw)
        u = y_ref[:, sl] * _silu(z_ref[:, sl])
        ms = jnp.sum(u * u, axis=-1, keepdims=True) * (1.0 / gw)
        u_ref[:, sl] = (u * lax.rsqrt(ms + EPS) * gout_ref[:, sl]).astype(u_ref.dtype)

    @pl.when(ck == n_ck - 1)
    def _():
        hout_ref[0] = h_ref[...]


def ssd_mixer(zxbc, dt_raw, bsz, t_len, w_conv, b_conv, dt_bias, a_log, d_skip, g_out):
    c = SSM_CHUNK
    n_ck = t_len // c
    xblk = SSM_INNER // SSM_BC
    pad = LANES - SSM_HEADS
    dtb = jnp.pad(dt_bias.astype(F32), (0, pad)).reshape(1, LANES)
    aneg = jnp.pad(-jnp.exp(a_log.astype(F32)), (0, pad)).reshape(1, LANES)
    dsk = jnp.repeat(d_skip.astype(F32), SSM_HEAD_DIM).reshape(1, SSM_INNER)
    row_spec = lambda width, blk: pl.BlockSpec((c, width), lambda i, j: (i * n_ck + j, blk))
    const = lambda shape: pl.BlockSpec(shape, lambda i, j: (0,) * len(shape))
    vmem = 2 * c * (2 * SSM_INNER + 2 * SSM_BC + LANES) * 4 + 2 * c * SSM_INNER * 2
    vmem += 3 * SSM_PAIRS * LANES * LANES * 4 + (c + 8) * SSM_XBC * 4 + c * SSM_XBC * 4 + c * SSM_INNER * 4
    vmem += 8 << 20
    u, h_out = pl.pallas_call(
        _ssd_kernel,
        grid=(bsz, n_ck),
        in_specs=[row_spec(SSM_INNER, 0), row_spec(SSM_INNER, 1),
                  row_spec(SSM_BC, 2 * xblk), row_spec(SSM_BC, 2 * xblk + 1),
                  row_spec(LANES, 0),
                  const((SSM_CONV, SSM_XBC)), const((1, SSM_XBC)), const((1, LANES)), const((1, LANES)),
                  const((1, SSM_INNER)), const((1, SSM_INNER))],
        out_specs=[pl.BlockSpec((c, SSM_INNER), lambda i, j: (i * n_ck + j, 0)),
                   pl.BlockSpec((1, SSM_PAIRS, LANES, LANES), lambda i, j: (i, 0, 0, 0))],
        out_shape=[jax.ShapeDtypeStruct((bsz * t_len, SSM_INNER), BF16),
                   jax.ShapeDtypeStruct((bsz, SSM_PAIRS, LANES, LANES), F32)],
        scratch_shapes=[pltpu.VMEM((c + 8, SSM_XBC), F32), pltpu.VMEM((c, SSM_XBC), F32),
                        pltpu.VMEM((SSM_PAIRS, LANES, LANES), F32), pltpu.VMEM((c, SSM_INNER), F32)],
        compiler_params=_cparams(("parallel", "arbitrary"), vmem),
        name="ssd_mixer",
    )(zxbc, zxbc, zxbc, zxbc, dt_raw, w_conv.astype(F32), b_conv.reshape(1, SSM_XBC).astype(F32),
      dtb, aneg, dsk, g_out.reshape(1, SSM_INNER).astype(F32))
    return u, h_out.reshape(bsz, SSM_HEADS, SSM_HEAD_DIM, SSM_STATE)


def _ssd_step_kernel(zxbc_ref, dt_ref, h_ref, conv_ref, wconv_ref, bconv_ref, dtb_ref, aneg_ref,
                     dskip_ref, gout_ref, u_ref, hout_ref, convout_ref, y_ref):
    xbc = zxbc_ref[0, :, SSM_INNER:SSM_INNER + SSM_XBC]
    prev = conv_ref[0]
    acc = bconv_ref[...] + wconv_ref[3:4, :] * xbc
    for tap in range(SSM_CONV - 1):
        acc = acc + wconv_ref[tap:tap + 1, :] * prev[tap:tap + 1, :]
    xs = _silu(acc)
    convout_ref[0, 0:2, :] = prev[1:3, :]
    convout_ref[0, 2:3, :] = xbc

    lane1 = lax.broadcasted_iota(jnp.int32, (1, LANES), 1)
    dt = jnp.where(lane1 < SSM_HEADS, _softplus(dt_ref[0] + dtb_ref[...]), 0.0)
    da = jnp.exp(dt * aneg_ref[...])
    lane = lax.broadcasted_iota(jnp.int32, (LANES, LANES), 1)
    row = lax.broadcasted_iota(jnp.int32, (LANES, LANES), 0)
    left1 = lane1 < SSM_HEAD_DIM
    top = row < SSM_HEAD_DIM
    for g in range(SSM_GROUPS):
        bg = xs[:, SSM_INNER + g * SSM_STATE:SSM_INNER + (g + 1) * SSM_STATE]
        cg = xs[:, SSM_INNER + SSM_BC + g * SSM_STATE:SSM_INNER + SSM_BC + (g + 1) * SSM_STATE]
        cb = jnp.sum(cg.astype(BF16).astype(F32) * bg.astype(BF16).astype(F32), axis=-1, keepdims=True)
        cg8 = jnp.broadcast_to(cg, (8, LANES)).astype(BF16)
        for q in range(SSM_PAIRS // SSM_GROUPS):
            pr = g * (SSM_PAIRS // SSM_GROUPS) + q
            h0, h1 = 2 * pr, 2 * pr + 1
            xpair = xs[:, pr * LANES:(pr + 1) * LANES]
            xdt = xpair * jnp.where(left1, dt[:, h0:h0 + 1], dt[:, h1:h1 + 1])
            xdt_col = jnp.broadcast_to(xdt, (LANES, LANES)).T
            hp = h_ref[0, pr]
            keep = jnp.where(top, da[:, h0:h0 + 1], da[:, h1:h1 + 1])
            hout_ref[0, pr] = hp * keep + xdt_col * bg
            y_in = lax.dot_general(cg8, hp.astype(BF16), (((1,), (1,)), ((), ())), preferred_element_type=F32)[0:1]
            y = cb * xdt + y_in * jnp.where(left1, da[:, h0:h0 + 1], da[:, h1:h1 + 1])
            y_ref[:, pr * LANES:(pr + 1) * LANES] = y + dskip_ref[:, pr * LANES:(pr + 1) * LANES] * xpair
    gw = SSM_INNER // SSM_GROUPS
    for g in range(SSM_GROUPS):
        sl = slice(g * gw, (g + 1) * gw)
        u = y_ref[:, sl] * _silu(zxbc_ref[0, :, sl])
        ms = jnp.sum(u * u, axis=-1, keepdims=True) * (1.0 / gw)
        u_ref[0, :, sl] = (u * lax.rsqrt(ms + EPS) * gout_ref[:, sl]).astype(u_ref.dtype)


def ssd_step(zxbc, dt_raw, h_prev, conv_prev, w_conv, b_conv, dt_bias, a_log, d_skip, g_out):
    bsz = zxbc.shape[0]
    pad = LANES - SSM_HEADS
    dtb = jnp.pad(dt_bias.astype(F32), (0, pad)).reshape(1, LANES)
    aneg = jnp.pad(-jnp.exp(a_log.astype(F32)), (0, pad)).reshape(1, LANES)
    dsk = jnp.repeat(d_skip.astype(F32), SSM_HEAD_DIM).reshape(1, SSM_INNER)
    const = lambda shape: pl.BlockSpec(shape, lambda i: (0,) * len(shape))
    st = (1, SSM_PAIRS, LANES, LANES)
    vmem = 4 * SSM_PAIRS * LANES * LANES * 4 + (8 << 20)
    u, h_out, conv_out = pl.pallas_call(
        _ssd_step_kernel,
        grid=(bsz,),
        in_specs=[pl.BlockSpec((1, 1, zxbc.shape[1]), lambda i: (i, 0, 0)),
                  pl.BlockSpec((1, 1, LANES), lambda i: (i, 0, 0)),
                  pl.BlockSpec(st, lambda i: (i, 0, 0, 0)),
                  pl.BlockSpec((1, SSM_CONV - 1, SSM_XBC), lambda i: (i, 0, 0)),
                  const((SSM_CONV, SSM_XBC)), const((1, SSM_XBC)), const((1, LANES)), const((1, LANES)),
                  const((1, SSM_INNER)), const((1, SSM_INNER))],
        out_specs=[pl.BlockSpec((1, 1, SSM_INNER), lambda i: (i, 0, 0)),
                   pl.BlockSpec(st, lambda i: (i, 0, 0, 0)),
                   pl.BlockSpec((1, SSM_CONV - 1, SSM_XBC), lambda i: (i, 0, 0))],
        out_shape=[jax.ShapeDtypeStruct((bsz, 1, SSM_INNER), BF16),
                   jax.ShapeDtypeStruct((bsz, SSM_PAIRS, LANES, LANES), F32),
                   jax.ShapeDtypeStruct((bsz, SSM_CONV - 1, SSM_XBC), F32)],
        scratch_shapes=[pltpu.VMEM((1, SSM_INNER), F32)],
        compiler_params=_cparams(("parallel",), vmem),
        name="ssd_step",
    )(zxbc.reshape(bsz, 1, -1), dt_raw.reshape(bsz, 1, LANES),
      h_prev.astype(F32).reshape(bsz, SSM_PAIRS, LANES, LANES), conv_prev.astype(F32),
      w_conv.astype(F32), b_conv.reshape(1, SSM_XBC).astype(F32), dtb, aneg, dsk,
      g_out.reshape(1, SSM_INNER).astype(F32))
    return u.reshape(bsz, SSM_INNER), h_out.reshape(bsz, SSM_HEADS, SSM_HEAD_DIM, SSM_STATE), conv_out


def _ffn_up_kernel(a_ref, wg_ref, wu_ref, wc_ref, bc_ref, o_ref, st_ref, gp_ref, *, tiles_per_seq, tm):
    i = pl.program_id(1)

    @pl.when(i % tiles_per_seq == 0)
    def _():
        gp_ref[0:8, :] = jnp.zeros((8, gp_ref.shape[1]), F32)

    a = a_ref[...]
    gate = jnp.dot(a, wg_ref[...].astype(BF16), preferred_element_type=F32)
    up = jnp.dot(a, wu_ref[...].astype(BF16), preferred_element_type=F32)
    gp_ref[8:8 + tm, :] = gate
    conv = bc_ref[...] + wc_ref[2:3, :] * gate
    conv = conv + wc_ref[1:2, :] * gp_ref[7:7 + tm, :]
    conv = conv + wc_ref[0:1, :] * gp_ref[6:6 + tm, :]
    o_ref[...] = (_silu(conv) * up).astype(o_ref.dtype)
    st_ref[0] = gp_ref[tm + 6:tm + 8, :]
    gp_ref[0:8, :] = gp_ref[tm:tm + 8, :]


def ffn_up(h, bsz, t_len, w_up, w_conv, b_conv, tm=1024, tn=256):
    m, d = h.shape
    d_ff = w_up.shape[1] // 2
    tm = min(tm, t_len)
    nj = d_ff // tn
    assert d_ff % tn == 0 and t_len % tm == 0 and tm % 8 == 0
    tps = t_len // tm
    vmem = 2 * tm * d * 2 + 4 * d * tn * 4 + 2 * d * tn * 2 + 2 * tm * tn * 2 + (tm + 8) * tn * 4 + 6 * tm * tn * 4
    kern = functools.partial(_ffn_up_kernel, tiles_per_seq=tps, tm=tm)
    return pl.pallas_call(
        kern,
        grid=(nj, m // tm),
        in_specs=[pl.BlockSpec((tm, d), lambda j, i: (i, 0)),
                  pl.BlockSpec((d, tn), lambda j, i: (0, j)),
                  pl.BlockSpec((d, tn), lambda j, i: (0, j + nj)),
                  pl.BlockSpec((FFN_CONV, tn), lambda j, i: (0, j)),
                  pl.BlockSpec((1, tn), lambda j, i: (0, j))],
        out_specs=[pl.BlockSpec((tm, tn), lambda j, i: (i, j)),
                   pl.BlockSpec((1, FFN_CONV - 1, tn), lambda j, i: (i // tps, 0, j))],
        out_shape=[jax.ShapeDtypeStruct((m, d_ff), BF16),
                   jax.ShapeDtypeStruct((bsz, FFN_CONV - 1, d_ff), F32)],
        scratch_shapes=[pltpu.VMEM((tm + 8, tn), F32)],
        compiler_params=_cparams(("parallel", "arbitrary"), vmem),
        name="ffn_up",
    )(h, w_up, w_up, w_conv.astype(F32), b_conv.reshape(1, d_ff).astype(F32))


def _ffn_step_kernel(gu_ref, st_ref, wc_ref, bc_ref, o_ref, stout_ref, *, d_ff):
    gate = gu_ref[:, 0:d_ff]
    up = gu_ref[:, d_ff:2 * d_ff]
    conv = bc_ref[...] + wc_ref[2:3, :] * gate + wc_ref[1:2, :] * st_ref[1] + wc_ref[0:1, :] * st_ref[0]
    o_ref[...] = (_silu(conv) * up).astype(o_ref.dtype)
    stout_ref[0] = st_ref[1]
    stout_ref[1] = gate


def ffn_step(gu, state, w_conv, b_conv):
    bsz, two_dff = gu.shape
    d_ff = two_dff // 2
    a, st = pl.pallas_call(
        functools.partial(_ffn_step_kernel, d_ff=d_ff),
        out_shape=[jax.ShapeDtypeStruct((bsz, d_ff), BF16),
                   jax.ShapeDtypeStruct((FFN_CONV - 1, bsz, d_ff), F32)],
        compiler_params=_cparams((), 16 << 20),
        name="ffn_step",
    )(gu, jnp.swapaxes(state.astype(F32), 0, 1), w_conv.astype(F32), b_conv.reshape(1, d_ff).astype(F32))
    return a, jnp.swapaxes(st, 0, 1)


def _dil_attn_kernel(q0_ref, q1_ref, q2_ref, k0_ref, k1_ref, k2_ref, v0_ref, v1_ref, v2_ref,
                     o_ref, m_ref, l_ref, acc_ref, *, t_len):
    q_refs, k_refs, v_refs = (q0_ref, q1_ref, q2_ref), (k0_ref, k1_ref, k2_ref), (v0_ref, v1_ref, v2_ref)
    scale = HEAD_DIM ** -0.5
    blk = 128
    ii = lax.broadcasted_iota(jnp.int32, (blk, blk), 0)
    jj = lax.broadcasted_iota(jnp.int32, (blk, blk), 1)
    cur_ok = jj <= ii
    prev_ok = jj >= ii
    nt = (((1,), (1,)), ((), ()))
    for g, (w, r) in enumerate(DIL_PATTERNS):
        assert w // r == blk
        n_blk = t_len // (r * blk)
        for cls in range(r):
            for qb in range(n_blk):
                rows = pl.ds(cls + qb * blk * r, blk, stride=r) if r > 1 else pl.ds(qb * blk, blk)
                q = q_refs[g][rows, :].astype(BF16)
                k = k_refs[g][rows, :].astype(BF16)
                v = v_refs[g][rows, :].astype(BF16)
                s = lax.dot_general(q, k, nt, preferred_element_type=F32) * scale
                s = jnp.where(cur_ok, s, -jnp.inf)
                m = jnp.max(s, axis=-1, keepdims=True)
                if qb > 0:
                    prow = (pl.ds(cls + (qb - 1) * blk * r, blk, stride=r) if r > 1
                            else pl.ds((qb - 1) * blk, blk))
                    kp = k_refs[g][prow, :].astype(BF16)
                    vp = v_refs[g][prow, :].astype(BF16)
                    sp = lax.dot_general(q, kp, nt, preferred_element_type=F32) * scale
                    sp = jnp.where(prev_ok, sp, -jnp.inf)
                    m = jnp.maximum(m, jnp.max(sp, axis=-1, keepdims=True))
                p = jnp.exp(s - m)
                den = jnp.sum(p, axis=-1, keepdims=True)
                acc = jnp.dot(p.astype(BF16), v, preferred_element_type=F32)
                if qb > 0:
                    pp = jnp.exp(sp - m)
                    den = den + jnp.sum(pp, axis=-1, keepdims=True)
                    acc = acc + jnp.dot(pp.astype(BF16), vp, preferred_element_type=F32)
                m = jnp.broadcast_to(m, (blk, HEAD_DIM))
                den = jnp.broadcast_to(den, (blk, HEAD_DIM))
                if g == 0:
                    m_ref[rows, :] = m
                    l_ref[rows, :] = den
                    acc_ref[rows, :] = acc
                else:
                    m_old = m_ref[rows, :]
                    m_new = jnp.maximum(m_old, m)
                    a_old = jnp.exp(m_old - m_new)
                    a_new = jnp.exp(m - m_new)
                    m_ref[rows, :] = m_new
                    l_ref[rows, :] = l_ref[rows, :] * a_old + den * a_new
                    acc_ref[rows, :] = acc_ref[rows, :] * a_old + acc * a_new
    o_ref[...] = (acc_ref[...] / l_ref[...]).astype(o_ref.dtype)


def dilated_attention_prompt(qq, kv, bsz, t_len):
    nh = DIL_HEADS
    tspec = lambda colfn: pl.BlockSpec((t_len, HEAD_DIM), lambda b, h: (b, colfn(h)))
    in_specs = ([tspec(lambda h, g=g: g * nh + h) for g in range(3)]
                + [tspec(lambda h, g=g: g * 2 * nh + h) for g in range(3)]
                + [tspec(lambda h, g=g: g * 2 * nh + nh + h) for g in range(3)])
    vmem = 2 * 9 * t_len * HEAD_DIM * 4 + 2 * t_len * HEAD_DIM * 2 + 3 * t_len * HEAD_DIM * 4 + (8 << 20)
    return pl.pallas_call(
        functools.partial(_dil_attn_kernel, t_len=t_len),
        grid=(bsz, nh),
        in_specs=in_specs,
        out_specs=pl.BlockSpec((t_len, HEAD_DIM), lambda b, h: (b, h)),
        out_shape=jax.ShapeDtypeStruct((bsz * t_len, nh * HEAD_DIM), BF16),
        scratch_shapes=[pltpu.VMEM((t_len, HEAD_DIM), F32)] * 3,
        compiler_params=_cparams(("parallel", "parallel"), vmem),
        name="dilated_attention",
    )(qq, qq, qq, kv, kv, kv, kv, kv, kv)


def _dil_step_kernel(q_ref, kvn_ref, c0_ref, c1_ref, c2_ref, o_ref):
    caches = (c0_ref, c1_ref, c2_ref)
    scale = HEAD_DIM ** -0.5
    nh = DIL_HEADS
    nt = (((1,), (1,)), ((), ()))
    for h in range(nh):
        s_past, s_new, v_past, v_new = [], [], [], []
        for g in range(3):
            qh = q_ref[0, :, (g * nh + h) * HEAD_DIM:(g * nh + h + 1) * HEAD_DIM]
            kn = kvn_ref[0, :, (g * 2 * nh + h) * HEAD_DIM:(g * 2 * nh + h + 1) * HEAD_DIM]
            vn = kvn_ref[0, :, (g * 2 * nh + nh + h) * HEAD_DIM:(g * 2 * nh + nh + h + 1) * HEAD_DIM]
            kp = caches[g][0, :, h * HEAD_DIM:(h + 1) * HEAD_DIM].astype(BF16)
            vp = caches[g][0, :, (nh + h) * HEAD_DIM:(nh + h + 1) * HEAD_DIM].astype(BF16)
            q8 = jnp.broadcast_to(qh, (8, HEAD_DIM)).astype(BF16)
            s_past.append(lax.dot_general(q8, kp, nt, preferred_element_type=F32)[0:1] * scale)
            s_new.append(jnp.sum(qh.astype(BF16).astype(F32) * kn.astype(BF16).astype(F32),
                                 axis=-1, keepdims=True) * scale)
            v_past.append(vp)
            v_new.append(vn)
        m = None
        for g in range(3):
            mg = jnp.maximum(jnp.max(s_past[g], axis=-1, keepdims=True), s_new[g])
            m = mg if m is None else jnp.maximum(m, mg)
        den = jnp.zeros((1, 1), F32)
        acc = jnp.zeros((1, HEAD_DIM), F32)
        for g in range(3):
            p = jnp.exp(s_past[g] - m)
            pn = jnp.exp(s_new[g] - m)
            den = den + jnp.sum(p, axis=-1, keepdims=True) + pn
            p8 = jnp.broadcast_to(p, (8, p.shape[1])).astype(BF16)
            acc = acc + jnp.dot(p8, v_past[g], preferred_element_type=F32)[0:1] + pn * v_new[g]
        o_ref[0, :, h * HEAD_DIM:(h + 1) * HEAD_DIM] = (acc / den).astype(o_ref.dtype)


def dilated_attention_step(qq, kv_new, caches):
    bsz = qq.shape[0]
    kvw = 2 * DIL_HEADS * HEAD_DIM
    views = []
    for cache, (w, r) in zip(caches, DIL_PATTERNS):
        assert cache.shape[1] == w, "rolling window cache must hold the full window"
        views.append(cache.reshape(bsz, w // r, r * kvw))
    row = lambda width: pl.BlockSpec((1, 1, width), lambda b: (b, 0, 0))
    out = pl.pallas_call(
        _dil_step_kernel,
        grid=(bsz,),
        in_specs=[row(3 * DIL_HEADS * HEAD_DIM), row(3 * kvw)]
                 + [pl.BlockSpec((1, w // r, kvw), lambda b: (b, 0, 0)) for (w, r) in DIL_PATTERNS],
        out_specs=row(DIL_HEADS * HEAD_DIM),
        out_shape=jax.ShapeDtypeStruct((bsz, 1, DIL_HEADS * HEAD_DIM), BF16),
        compiler_params=_cparams(("parallel",), 2 * 3 * 128 * kvw * 4 + (8 << 20)),
        name="dilated_attention_step",
    )(qq[:, :3 * DIL_HEADS * HEAD_DIM].reshape(bsz, 1, -1), kv_new.reshape(bsz, 1, -1), *views)
    return out.reshape(bsz, -1)


def _mm_tiles(m):
    return dict(tm=min(m, 1024), tn=1024, tk=1024)


def _trunk(x, bsz, t_len, pos, mem_kv, ssm_prev, conv_prev, ffn_prev, win_past, p):
    m, d = x.shape
    prompt = ssm_prev is None
    tiles = _mm_tiles(m)
    rope = rope_tables(pos) if prompt else tuple(jnp.broadcast_to(t, (m, HEAD_DIM)) for t in rope_tables(pos))
    ffn_states = []

    def conv_ffn(x, i):
        h = rmsnorm_rows(x, p["g_ffn"][i])
        if prompt:
            a, st = ffn_up(h, bsz, t_len, p["w_ffn_up"][i], p["w_ffn_conv"][i], p["b_ffn_conv"][i])
        else:
            gu = matmul(h, p["w_ffn_up"][i], n=p["w_ffn_up"].shape[2], tm=m, tn=512, tk=4096, name="ffn_up_raw")
            a, st = ffn_step(gu, ffn_prev[i], p["w_ffn_conv"][i], p["b_ffn_conv"][i])
        ffn_states.append(st)
        return matmul(a, p["w_ffn_down"][i], n=d, res=x, name="ffn_down", **tiles)

    w_in = p["w_in_a"][0]
    zx_cols = SSM_INNER + SSM_XBC
    h = rmsnorm_rows(x, p["g_mix"][0])
    zxbc = matmul(h, w_in, n=zx_cols, name="in_proj_a", **tiles)
    dt_raw = matmul(h, w_in, n=LANES, w_col0=zx_cols, tm=tiles["tm"], tn=LANES, tk=4096, name="in_proj_dt")
    qm = matmul(h, w_in[:, zx_cols + SSM_HEADS:], n=MEM_W, modes=["norm256"],
                gain=jnp.tile(p["g_mem_q"][0], MEM_HEADS), name="in_proj_qmem", **tiles)
    y_mem = mem_attention(qm.reshape(bsz, t_len, MEM_W), 0, mem_kv[0])
    ssm_args = (p["w_conv_a"][0], p["b_conv_a"][0], p["dt_bias_a"][0], p["a_log_a"][0], p["d_skip_a"][0],
                p["g_ssm_out_a"][0])
    if prompt:
        u, ssm_new = ssd_mixer(zxbc, dt_raw, bsz, t_len, *ssm_args)
        conv_new = zxbc.reshape(bsz, t_len, zx_cols)[:, t_len - (SSM_CONV - 1):, SSM_INNER:]
    else:
        u, ssm_new, conv_new = ssd_step(zxbc, dt_raw, ssm_prev[0], conv_prev[0], *ssm_args)
    mix = jnp.concatenate([u, y_mem.reshape(m, MEM_W)], axis=1)
    x = matmul(mix, p["w_out_a"][0], n=d, res=x, name="out_proj_a", **tiles)
    x = conv_ffn(x, 0)

    nh = DIL_HEADS
    gk = jnp.concatenate([jnp.concatenate([jnp.tile(p["g_k_dil"][g], nh), jnp.ones((nh * HEAD_DIM,), F32)])
                          for g in range(3)])
    hk = rmsnorm_rows(x, p["g_kv"])
    kv = matmul(hk, p["w_kv"], n=6 * nh * HEAD_DIM, modes=["norm128rope", "plain"] * 3, gain=gk, rope=rope,
                name="kv_proj", **tiles)

    gq = jnp.concatenate([jnp.tile(p["g_q_dil"][0][g], nh) for g in range(3)]
                         + [jnp.tile(p["g_mem_q"][1], MEM_HEADS)])
    h = rmsnorm_rows(x, p["g_mix"][1])
    qq = matmul(h, p["w_in_b"][0], n=d, modes=["norm128rope"] * 3 + ["norm256"], gain=gq, rope=rope,
                name="in_proj_b", **tiles)
    y_mem = mem_attention(qq.reshape(bsz, t_len, d), 3, mem_kv[1])
    if prompt:
        att = dilated_attention_prompt(qq, kv, bsz, t_len)
    else:
        att = dilated_attention_step(qq, kv, win_past)
    mix = jnp.concatenate([att, y_mem.reshape(m, MEM_W)], axis=1)
    x = matmul(mix, p["w_out_b"][0], n=d, res=x, name="out_proj_b", **tiles)
    x = conv_ffn(x, 1)
    return x, ssm_new, conv_new, jnp.stack(ffn_states, axis=0), kv


def _memory_kv(mem, g_norm, w_kv, g_k):
    bsz, n, d = mem.shape
    hm = rmsnorm_rows(mem.reshape(bsz * n, d), g_norm)
    gain = jnp.concatenate([jnp.tile(g_k, MEM_HEADS), jnp.ones((MEM_W,), F32)])
    kv = matmul(hm, w_kv, n=2 * MEM_W, modes=["norm256", "plain"], gain=gain, name="mem_kv_proj",
                **_mm_tiles(bsz * n))
    return kv.reshape(bsz, n, 2 * MEM_W)


def kernel(x_prompt, x_sample, state_ssm, state_ssm_conv, state_ffn_conv, cache_mem_kv, cache_win_kv0, cache_win_kv1, cache_win_kv2, mem_prompt, g_mix, w_in_a, w_conv_a, b_conv_a, dt_bias_a, a_log_a, d_skip_a, g_ssm_out_a, w_out_a, g_kv, w_kv, g_k_dil, w_in_b, g_q_dil, w_out_b, g_mem, w_mem_kv, g_mem_q, g_mem_k, g_ffn, w_ffn_up, w_ffn_conv, b_ffn_conv, w_ffn_down):
    p = dict(g_mix=g_mix, w_in_a=w_in_a, w_conv_a=w_conv_a, b_conv_a=b_conv_a, dt_bias_a=dt_bias_a,
             a_log_a=a_log_a, d_skip_a=d_skip_a, g_ssm_out_a=g_ssm_out_a, w_out_a=w_out_a, g_kv=g_kv, w_kv=w_kv,
             g_k_dil=g_k_dil, w_in_b=w_in_b, g_q_dil=g_q_dil, w_out_b=w_out_b, g_mem_q=g_mem_q, g_ffn=g_ffn,
             w_ffn_up=w_ffn_up, w_ffn_conv=w_ffn_conv, b_ffn_conv=b_ffn_conv, w_ffn_down=w_ffn_down)
    bp, sp, d = x_prompt.shape
    bs, ds, _ = x_sample.shape
    depth = g_mix.shape[0]
    nh = DIL_HEADS

    mem_kv_p = [_memory_kv(mem_prompt, g_mem[i], w_mem_kv[i], g_mem_k[i]) for i in range(depth)]
    y_p, ssm_p, conv_p, ffn_p, kv_p = _trunk(x_prompt.reshape(bp * sp, d), bp, sp, jnp.arange(sp, dtype=jnp.int32),
                                             mem_kv_p, None, None, None, None, p)
    assert ds == 1
    mem_kv_s = [cache_mem_kv[i].reshape(bs, N_MEM, 2 * MEM_W) for i in range(depth)]
    pos_s = PAST_LEN + jnp.arange(ds, dtype=jnp.int32)
    y_s, ssm_s, conv_s, ffn_s, kv_s = _trunk(x_sample.reshape(bs * ds, d), bs, ds, pos_s, mem_kv_s, state_ssm,
                                             state_ssm_conv, state_ffn_conv,
                                             [cache_win_kv0, cache_win_kv1, cache_win_kv2], p)

    kv_p = kv_p.reshape(bp, sp, 3, 2, nh, HEAD_DIM)
    kv_s = kv_s.reshape(bs, ds, 3, 2, nh, HEAD_DIM)
    win_p = [kv_p[:, sp - min(w, sp):, g] for g, (w, _) in enumerate(DIL_PATTERNS)]
    mem_out = jnp.stack(mem_kv_p, axis=0).reshape(depth, bp, N_MEM, 2, MEM_HEADS, MEM_HEAD_DIM)
    return (y_p.reshape(bp, sp, d), y_s.reshape(bs, ds, d), ssm_p[None], ssm_s[None], conv_p[None], conv_s[None],
            ffn_p, ffn_s, mem_out, win_p[0], win_p[1], win_p[2], kv_s[:, :, 0], kv_s[:, :, 1], kv_s[:, :, 2])
```

```python
import functools
import math

import jax
import jax.numpy as jnp
from jax import lax
from jax.experimental import pallas as pl
from jax.experimental.pallas import tpu as pltpu

F32 = jnp.float32
BF16 = jnp.bfloat16
EPS = 1e-6
LANES = 128
V7X_VMEM_BYTES = 64 * 2**20

SSM_HEAD_DIM = 64
SSM_HEADS = 48
SSM_GROUPS = 8
SSM_STATE = 128
SSM_INNER = SSM_HEADS * SSM_HEAD_DIM
SSM_BC = SSM_GROUPS * SSM_STATE
SSM_XBC = SSM_INNER + 2 * SSM_BC
SSM_CONV = 4
SSM_CHUNK = 128
SSM_PAIRS = SSM_HEADS // 2
DIL_PATTERNS = ((128, 1), (512, 4), (2048, 16))
DIL_HEADS = 8
HEAD_DIM = 128
ROT_DIM = HEAD_DIM // 4
ROPE_THETA = 500000.0
N_MEM = 256
MEM_HEADS = 4
MEM_HEAD_DIM = 256
MEM_W = MEM_HEADS * MEM_HEAD_DIM
FFN_CONV = 3
PAST_LEN = 16384


def _cparams(sem, vmem_bytes):
    limit = min(int(vmem_bytes * 1.25) + (4 << 20), V7X_VMEM_BYTES - (6 << 20))
    return pltpu.CompilerParams(dimension_semantics=sem or None, vmem_limit_bytes=limit)


def _silu(x):
    return x * jax.nn.sigmoid(x)


def _softplus(x):
    return jnp.maximum(x, 0.0) + jnp.log1p(jnp.exp(-jnp.abs(x)))


def _rmsnorm_kernel(x_ref, g_ref, o_ref):
    x = x_ref[...]
    ms = jnp.mean(x * x, axis=-1, keepdims=True)
    o_ref[...] = (x * lax.rsqrt(ms + EPS) * g_ref[...]).astype(o_ref.dtype)


def rmsnorm_rows(x, g, tm=256):
    m, d = x.shape
    tm = min(tm, m)
    return pl.pallas_call(
        _rmsnorm_kernel,
        grid=(m // tm,),
        in_specs=[pl.BlockSpec((tm, d), lambda i: (i, 0)),
                  pl.BlockSpec((1, d), lambda i: (0, 0))],
        out_specs=pl.BlockSpec((tm, d), lambda i: (i, 0)),
        out_shape=jax.ShapeDtypeStruct((m, d), BF16),
        compiler_params=_cparams(("parallel",), 2 * tm * d * 6),
        name="rmsnorm_rows",
    )(x, g.reshape(1, d).astype(F32))


def rope_tables(pos):
    half = ROT_DIM // 2
    inv_freq = jnp.exp(-(2.0 * jnp.arange(half, dtype=F32) / ROT_DIM) * math.log(ROPE_THETA))
    ang = pos.astype(F32)[:, None] * inv_freq[None, :]
    cos, sin = jnp.cos(ang), jnp.sin(ang)
    n = pos.shape[0]
    ones = jnp.ones((n, HEAD_DIM - ROT_DIM), F32)
    zeros = jnp.zeros((n, HEAD_DIM - ROT_DIM), F32)
    zh = jnp.zeros((n, half), F32)
    c = jnp.concatenate([cos, cos, ones], axis=1)
    s_lo = jnp.concatenate([-sin, zh, zeros], axis=1)
    s_hi = jnp.concatenate([zh, sin, zeros], axis=1)
    return c, s_lo, s_hi


def _head_norm(blk, gain, hd):
    ms = jnp.sum(blk * blk, axis=-1, keepdims=True) * (1.0 / hd)
    return blk * lax.rsqrt(ms + EPS) * gain


def _mm_kernel(*refs, n_a, modes, has_gain, has_rope, has_res, tn):
    it = iter(refs)
    a_refs = [next(it) for _ in range(n_a)]
    w_refs = [next(it) for _ in range(n_a)]
    gain_ref = next(it) if has_gain else None
    rope_refs = (next(it), next(it), next(it)) if has_rope else None
    res_ref = next(it) if has_res else None
    o_ref = next(it)
    j = pl.program_id(1)

    p = None
    for a_ref, w_ref in zip(a_refs, w_refs):
        d = jnp.dot(a_ref[...].astype(BF16), w_ref[...].astype(BF16), preferred_element_type=F32)
        p = d if p is None else p + d

    def epilogue(mode):
        if mode == "plain":
            y = p + res_ref[...] if has_res else p
            o_ref[...] = y.astype(o_ref.dtype)
            return
        hd = 256 if mode == "norm256" else 128
        for h0 in range(0, tn, hd):
            y = _head_norm(p[:, h0:h0 + hd], gain_ref[:, h0:h0 + hd], hd)
            if mode == "norm128rope":
                c_ref, slo_ref, shi_ref = rope_refs
                half = ROT_DIM // 2
                y = (y * c_ref[...]
                     + pltpu.roll(y, hd - half, 1) * slo_ref[...]
                     + pltpu.roll(y, half, 1) * shi_ref[...])
            o_ref[:, h0:h0 + hd] = y.astype(o_ref.dtype)

    distinct = sorted(set(modes))
    if len(distinct) == 1:
        epilogue(distinct[0])
    else:
        for mode in distinct:
            cond = None
            for jj, mj in enumerate(modes):
                if mj == mode:
                    c = j == jj
                    cond = c if cond is None else jnp.logical_or(cond, c)
            pl.when(cond)(functools.partial(epilogue, mode))


def matmul(a_list, w, layer, *, n, tm=1024, tn=512, w_col0=0, out_dtype=F32, modes=None, gain=None,
           rope=None, res=None, name="matmul"):
    m = a_list[0].shape[0]
    tm = min(tm, m)
    assert m % tm == 0 and n % tn == 0 and w_col0 % tn == 0
    nj = n // tn
    modes = tuple(modes) if modes is not None else ("plain",) * nj
    assert len(modes) == nj
    cb = w_col0 // tn
    in_specs, w_specs, vmem, row0 = [], [], 0, 0
    for a in a_list:
        ki = a.shape[1]
        assert row0 % ki == 0
        rb = row0 // ki
        in_specs.append(pl.BlockSpec((tm, ki), lambda i, j: (i, 0), pipeline_mode=pl.Buffered(1)))
        w_specs.append(pl.BlockSpec((None, ki, tn), lambda i, j, rb=rb: (layer, rb, j + cb)))
        vmem += tm * ki * a.dtype.itemsize + tm * ki * 2 + 2 * ki * tn * 4 + ki * tn * 2
        row0 += ki
    assert row0 == w.shape[1]
    in_specs += w_specs
    args = list(a_list) + [w] * len(a_list)
    vmem += 2 * tm * tn * jnp.dtype(out_dtype).itemsize + 2 * tm * tn * 4
    if gain is not None:
        in_specs.append(pl.BlockSpec((1, tn), lambda i, j: (0, j)))
        args.append(gain.reshape(1, n).astype(F32))
    if rope is not None:
        period = rope[0].shape[0]
        assert period % tm == 0
        nper = period // tm
        for t in rope:
            in_specs.append(pl.BlockSpec((tm, HEAD_DIM), lambda i, j: (i % nper, 0)))
            args.append(t)
    if res is not None:
        in_specs.append(pl.BlockSpec((tm, tn), lambda i, j: (i, j)))
        args.append(res)
        vmem += 2 * tm * tn * 4
    kern = functools.partial(_mm_kernel, n_a=len(a_list), modes=modes, has_gain=gain is not None,
                             has_rope=rope is not None, has_res=res is not None, tn=tn)
    return pl.pallas_call(
        kern,
        grid=(m // tm, nj),
        in_specs=in_specs,
        out_specs=pl.BlockSpec((tm, tn), lambda i, j: (i, j)),
        out_shape=jax.ShapeDtypeStruct((m, n), out_dtype),
        compiler_params=_cparams(("parallel", "arbitrary"), vmem),
        name=name,
    )(*args)


def _mm_panel_kernel(a_ref, w_ref, res_ref, o_ref, acc_ref, *, nk):
    k, j = pl.program_id(1), pl.program_id(2)
    p = jnp.dot(a_ref[...].astype(BF16), w_ref[...].astype(BF16), preferred_element_type=F32)

    @pl.when(k == 0)
    def _():
        acc_ref[j] = p

    if nk > 2:
        @pl.when(jnp.logical_and(k > 0, k < nk - 1))
        def _():
            acc_ref[j] += p

    @pl.when(k == nk - 1)
    def _():
        o_ref[...] = (acc_ref[j] + p + res_ref[...]).astype(o_ref.dtype)


def matmul_long_k(a, w, layer, res, *, tm=1024, tn=256, nk=2, name="matmul_long_k"):
    m, kdim = a.shape
    n = w.shape[2]
    tm = min(tm, m)
    assert kdim % nk == 0 and (kdim // nk) % LANES == 0 and m % tm == 0 and n % tn == 0 and nk >= 2
    tk = kdim // nk
    nj = n // tn
    last = lambda k, j: jnp.where(k == nk - 1, j, 0)
    vmem = tm * tk * 2 + 2 * tk * tn * 4 + tk * tn * 2 + nj * tm * tn * 4 + 5 * tm * tn * 4
    return pl.pallas_call(
        functools.partial(_mm_panel_kernel, nk=nk),
        grid=(m // tm, nk, nj),
        in_specs=[pl.BlockSpec((tm, tk), lambda i, k, j: (i, k), pipeline_mode=pl.Buffered(1)),
                  pl.BlockSpec((None, tk, tn), lambda i, k, j: (layer, k, j)),
                  pl.BlockSpec((tm, tn), lambda i, k, j: (i, last(k, j)))],
        out_specs=pl.BlockSpec((tm, tn), lambda i, k, j: (i, last(k, j))),
        out_shape=jax.ShapeDtypeStruct((m, n), F32),
        scratch_shapes=[pltpu.VMEM((nj, tm, tn), F32)],
        compiler_params=_cparams(("parallel", "arbitrary", "arbitrary"), vmem),
        name=name,
    )(a, w, res)


def _mem_attn_kernel(q_ref, kv_ref, o_ref):
    scale = MEM_HEAD_DIM ** -0.5
    for h in range(MEM_HEADS):
        lo = h * MEM_HEAD_DIM
        q = q_ref[0, :, lo:lo + MEM_HEAD_DIM].astype(BF16)
        tq = q.shape[0]
        if tq < 8:
            q = jnp.broadcast_to(q, (8, MEM_HEAD_DIM))
        k = kv_ref[0, :, lo:lo + MEM_HEAD_DIM].astype(BF16)
        v = kv_ref[0, :, MEM_W + lo:MEM_W + lo + MEM_HEAD_DIM].astype(BF16)
        s = lax.dot_general(q, k, (((1,), (1,)), ((), ())), preferred_element_type=F32) * scale
        p = jnp.exp(s - jnp.max(s, axis=-1, keepdims=True))
        den = jnp.sum(p, axis=-1, keepdims=True)
        o = jnp.dot(p.astype(BF16), v, preferred_element_type=F32) / den
        o_ref[0, :, lo:lo + MEM_HEAD_DIM] = o[0:tq].astype(o_ref.dtype)


def mem_attention(q, q_colblk, mem_kv, tq=512):
    b, t, _ = q.shape
    tq = min(tq, t)
    return pl.pallas_call(
        _mem_attn_kernel,
        grid=(b, t // tq),
        in_specs=[pl.BlockSpec((1, tq, MEM_W), lambda i, j: (i, j, q_colblk)),
                  pl.BlockSpec((1, N_MEM, 2 * MEM_W), lambda i, j: (i, 0, 0))],
        out_specs=pl.BlockSpec((1, tq, MEM_W), lambda i, j: (i, j, 0)),
        out_shape=jax.ShapeDtypeStruct((b, t, MEM_W), BF16),
        compiler_params=_cparams(("parallel", "parallel"), 2 * (tq * MEM_W * 6 + N_MEM * 2 * MEM_W * 4) + 8 * tq * N_MEM * 4),
        name="mem_attention",
    )(q, mem_kv)


def _split3_dot(lhs_bf16, x):
    hi = x.astype(BF16)
    r1 = x - hi.astype(F32)
    mid = r1.astype(BF16)
    lo = (r1 - mid.astype(F32)).astype(BF16)
    out = jnp.dot(lhs_bf16, hi, preferred_element_type=F32)
    out += jnp.dot(lhs_bf16, mid, preferred_element_type=F32)
    out += jnp.dot(lhs_bf16, lo, preferred_element_type=F32)
    return out


def _ssd_kernel(z_ref, x_ref, b_ref, c_ref, dt_ref, wconv_ref, bconv_ref, dtb_ref, aneg_ref,
                dskip_ref, gout_ref, u_ref, hout_ref, xp_ref, xs_ref, h_ref, y_ref):
    ck = pl.program_id(1)
    n_ck = pl.num_programs(1)
    c = SSM_CHUNK

    @pl.when(ck == 0)
    def _():
        xp_ref[0:8, :] = jnp.zeros((8, SSM_XBC), F32)
        h_ref[...] = jnp.zeros_like(h_ref)

    xp_ref[8:8 + c, 0:SSM_INNER] = x_ref[...]
    xp_ref[8:8 + c, SSM_INNER:SSM_INNER + SSM_BC] = b_ref[...]
    xp_ref[8:8 + c, SSM_INNER + SSM_BC:SSM_XBC] = c_ref[...]

    slab = 512
    for c0 in range(0, SSM_XBC, slab):
        acc = bconv_ref[:, c0:c0 + slab] + wconv_ref[3:4, c0:c0 + slab] * xp_ref[8:8 + c, c0:c0 + slab]
        for tap in range(SSM_CONV - 1):
            off = 8 - (SSM_CONV - 1 - tap)
            acc = acc + wconv_ref[tap:tap + 1, c0:c0 + slab] * xp_ref[off:off + c, c0:c0 + slab]
        xs_ref[:, c0:c0 + slab] = _silu(acc)
    xp_ref[0:8, :] = xp_ref[c:c + 8, :]

    lane = lax.broadcasted_iota(jnp.int32, (c, LANES), 1)
    row = lax.broadcasted_iota(jnp.int32, (c, LANES), 0)
    dt = jnp.where(lane < SSM_HEADS, _softplus(dt_ref[...] + dtb_ref[...]), 0.0)
    la = dt * aneg_ref[...]
    tri = row >= lane
    cum = _split3_dot(jnp.where(tri, 1.0, 0.0).astype(BF16), la)
    cum_t = cum.T
    ecum = jnp.exp(cum)
    wend = jnp.exp(cum[c - 1:c, :] - cum)
    elast_t = jnp.exp(cum_t[:, c - 1:c])
    left = lane < SSM_HEAD_DIM
    top = row < SSM_HEAD_DIM

    def pair(col0, col1):
        return jnp.where(left, col0, col1)

    for g in range(SSM_GROUPS):
        bg = xs_ref[:, SSM_INNER + g * SSM_STATE:SSM_INNER + (g + 1) * SSM_STATE].astype(BF16)
        cg = xs_ref[:, SSM_INNER + SSM_BC + g * SSM_STATE:SSM_INNER + SSM_BC + (g + 1) * SSM_STATE].astype(BF16)
        cb = lax.dot_general(cg, bg, (((1,), (1,)), ((), ())), preferred_element_type=F32)
        for q in range(SSM_PAIRS // SSM_GROUPS):
            pr = g * (SSM_PAIRS // SSM_GROUPS) + q
            h0, h1 = 2 * pr, 2 * pr + 1
            xpair = xs_ref[:, pr * LANES:(pr + 1) * LANES]
            xdt = xpair * pair(dt[:, h0:h0 + 1], dt[:, h1:h1 + 1])
            xdt_b = xdt.astype(BF16)
            ys = []
            for hh in (h0, h1):
                seg = cum[:, hh:hh + 1] - cum_t[hh:hh + 1, :]
                decay = jnp.exp(jnp.where(tri, seg, -jnp.inf))
                ys.append(jnp.dot((cb * decay).astype(BF16), xdt_b, preferred_element_type=F32))
            y = pair(ys[0], ys[1])
            hp = h_ref[pr]
            y_in = lax.dot_general(cg, hp.astype(BF16), (((1,), (1,)), ((), ())), preferred_element_type=F32)
            y = y + y_in * pair(ecum[:, h0:h0 + 1], ecum[:, h1:h1 + 1])
            xw = xdt * pair(wend[:, h0:h0 + 1], wend[:, h1:h1 + 1])
            upd = jnp.dot(xw.T.astype(BF16), bg, preferred_element_type=F32)
            keep = jnp.where(top, elast_t[h0:h0 + 1, :], elast_t[h1:h1 + 1, :])
            h_ref[pr] = hp * keep + upd
            y_ref[:, pr * LANES:(pr + 1) * LANES] = y + dskip_ref[:, pr * LANES:(pr + 1) * LANES] * xpair

    gw = SSM_INNER // SSM_GROUPS
    for g in range(SSM_GROUPS):
        sl = slice(g * gw, (g + 1) * gw)
        u = y_ref[:, sl] * _silu(z_ref[:, sl])
        ms = jnp.sum(u * u, axis=-1, keepdims=True) * (1.0 / gw)
        u_ref[:, sl] = (u * lax.rsqrt(ms + EPS) * gout_ref[:, sl]).astype(u_ref.dtype)

    @pl.when(ck == n_ck - 1)
    def _():
        hout_ref[0] = h_ref[...]


def ssd_mixer(zxbc, dt_raw, bsz, t_len, w_conv, b_conv, dt_bias, a_log, d_skip, g_out):
    c = SSM_CHUNK
    n_ck = t_len // c
    xblk = SSM_INNER // SSM_BC
    pad = LANES - SSM_HEADS
    dtb = jnp.pad(dt_bias.astype(F32), (0, pad)).reshape(1, LANES)
    aneg = jnp.pad(-jnp.exp(a_log.astype(F32)), (0, pad)).reshape(1, LANES)
    dsk = jnp.repeat(d_skip.astype(F32), SSM_HEAD_DIM).reshape(1, SSM_INNER)
    row_spec = lambda width, blk: pl.BlockSpec((c, width), lambda i, j: (i * n_ck + j, blk))
    const = lambda shape: pl.BlockSpec(shape, lambda i, j: (0,) * len(shape))
    vmem = 2 * c * (2 * SSM_INNER + 2 * SSM_BC + LANES) * 4 + 2 * c * SSM_INNER * 2
    vmem += 3 * SSM_PAIRS * LANES * LANES * 4 + (c + 8) * SSM_XBC * 4 + c * SSM_XBC * 4 + c * SSM_INNER * 4
    vmem += 8 << 20
    u, h_out = pl.pallas_call(
        _ssd_kernel,
        grid=(bsz, n_ck),
        in_specs=[row_spec(SSM_INNER, 0), row_spec(SSM_INNER, 1),
                  row_spec(SSM_BC, 2 * xblk), row_spec(SSM_BC, 2 * xblk + 1),
                  row_spec(LANES, 0),
                  const((SSM_CONV, SSM_XBC)), const((1, SSM_XBC)), const((1, LANES)), const((1, LANES)),
                  const((1, SSM_INNER)), const((1, SSM_INNER))],
        out_specs=[pl.BlockSpec((c, SSM_INNER), lambda i, j: (i * n_ck + j, 0)),
                   pl.BlockSpec((1, SSM_PAIRS, LANES, LANES), lambda i, j: (i, 0, 0, 0))],
        out_shape=[jax.ShapeDtypeStruct((bsz * t_len, SSM_INNER), BF16),
                   jax.ShapeDtypeStruct((bsz, SSM_PAIRS, LANES, LANES), F32)],
        scratch_shapes=[pltpu.VMEM((c + 8, SSM_XBC), F32), pltpu.VMEM((c, SSM_XBC), F32),
                        pltpu.VMEM((SSM_PAIRS, LANES, LANES), F32), pltpu.VMEM((c, SSM_INNER), F32)],
        compiler_params=_cparams(("parallel", "arbitrary"), vmem),
        name="ssd_mixer",
    )(zxbc, zxbc, zxbc, zxbc, dt_raw, w_conv.astype(F32), b_conv.reshape(1, SSM_XBC).astype(F32),
      dtb, aneg, dsk, g_out.reshape(1, SSM_INNER).astype(F32))
    return u, h_out.reshape(bsz, SSM_HEADS, SSM_HEAD_DIM, SSM_STATE)


def _ssd_step_kernel(zxbc_ref, dt_ref, h_ref, conv_ref, wconv_ref, bconv_ref, dtb_ref, aneg_ref,
                     dskip_ref, gout_ref, u_ref, hout_ref, convout_ref, y_ref):
    xbc = zxbc_ref[0, :, SSM_INNER:SSM_INNER + SSM_XBC]
    prev = conv_ref[0]
    acc = bconv_ref[...] + wconv_ref[3:4, :] * xbc
    for tap in range(SSM_CONV - 1):
        acc = acc + wconv_ref[tap:tap + 1, :] * prev[tap:tap + 1, :]
    xs = _silu(acc)
    convout_ref[0, 0:2, :] = prev[1:3, :]
    convout_ref[0, 2:3, :] = xbc

    lane1 = lax.broadcasted_iota(jnp.int32, (1, LANES), 1)
    dt = jnp.where(lane1 < SSM_HEADS, _softplus(dt_ref[0] + dtb_ref[...]), 0.0)
    da = jnp.exp(dt * aneg_ref[...])
    lane = lax.broadcasted_iota(jnp.int32, (LANES, LANES), 1)
    row = lax.broadcasted_iota(jnp.int32, (LANES, LANES), 0)
    left1 = lane1 < SSM_HEAD_DIM
    top = row < SSM_HEAD_DIM
    for g in range(SSM_GROUPS):
        bg = xs[:, SSM_INNER + g * SSM_STATE:SSM_INNER + (g + 1) * SSM_STATE]
        cg = xs[:, SSM_INNER + SSM_BC + g * SSM_STATE:SSM_INNER + SSM_BC + (g + 1) * SSM_STATE]
        cb = jnp.sum(cg.astype(BF16).astype(F32) * bg.astype(BF16).astype(F32), axis=-1, keepdims=True)
        cg8 = jnp.broadcast_to(cg, (8, LANES)).astype(BF16)
        for q in range(SSM_PAIRS // SSM_GROUPS):
            pr = g * (SSM_PAIRS // SSM_GROUPS) + q
            h0, h1 = 2 * pr, 2 * pr + 1
            xpair = xs[:, pr * LANES:(pr + 1) * LANES]
            xdt = xpair * jnp.where(left1, dt[:, h0:h0 + 1], dt[:, h1:h1 + 1])
            xdt_col = jnp.broadcast_to(xdt, (LANES, LANES)).T
            hp = h_ref[0, pr]
            keep = jnp.where(top, da[:, h0:h0 + 1], da[:, h1:h1 + 1])
            hout_ref[0, pr] = hp * keep + xdt_col * bg
            y_in = lax.dot_general(cg8, hp.astype(BF16), (((1,), (1,)), ((), ())), preferred_element_type=F32)[0:1]
            y = cb * xdt + y_in * jnp.where(left1, da[:, h0:h0 + 1], da[:, h1:h1 + 1])
            y_ref[:, pr * LANES:(pr + 1) * LANES] = y + dskip_ref[:, pr * LANES:(pr + 1) * LANES] * xpair
    gw = SSM_INNER // SSM_GROUPS
    for g in range(SSM_GROUPS):
        sl = slice(g * gw, (g + 1) * gw)
        u = y_ref[:, sl] * _silu(zxbc_ref[0, :, sl])
        ms = jnp.sum(u * u, axis=-1, keepdims=True) * (1.0 / gw)
        u_ref[0, :, sl] = (u * lax.rsqrt(ms + EPS) * gout_ref[:, sl]).astype(u_ref.dtype)


def ssd_step(zxbc, dt_raw, h_prev, conv_prev, w_conv, b_conv, dt_bias, a_log, d_skip, g_out):
    bsz = zxbc.shape[0]
    pad = LANES - SSM_HEADS
    dtb = jnp.pad(dt_bias.astype(F32), (0, pad)).reshape(1, LANES)
    aneg = jnp.pad(-jnp.exp(a_log.astype(F32)), (0, pad)).reshape(1, LANES)
    dsk = jnp.repeat(d_skip.astype(F32), SSM_HEAD_DIM).reshape(1, SSM_INNER)
    const = lambda shape: pl.BlockSpec(shape, lambda i: (0,) * len(shape))
    st = (1, SSM_PAIRS, LANES, LANES)
    vmem = 4 * SSM_PAIRS * LANES * LANES * 4 + (8 << 20)
    u, h_out, conv_out = pl.pallas_call(
        _ssd_step_kernel,
        grid=(bsz,),
        in_specs=[pl.BlockSpec((1, 1, zxbc.shape[1]), lambda i: (i, 0, 0)),
                  pl.BlockSpec((1, 1, LANES), lambda i: (i, 0, 0)),
                  pl.BlockSpec(st, lambda i: (i, 0, 0, 0)),
                  pl.BlockSpec((1, SSM_CONV - 1, SSM_XBC), lambda i: (i, 0, 0)),
                  const((SSM_CONV, SSM_XBC)), const((1, SSM_XBC)), const((1, LANES)), const((1, LANES)),
                  const((1, SSM_INNER)), const((1, SSM_INNER))],
        out_specs=[pl.BlockSpec((1, 1, SSM_INNER), lambda i: (i, 0, 0)),
                   pl.BlockSpec(st, lambda i: (i, 0, 0, 0)),
                   pl.BlockSpec((1, SSM_CONV - 1, SSM_XBC), lambda i: (i, 0, 0))],
        out_shape=[jax.ShapeDtypeStruct((bsz, 1, SSM_INNER), BF16),
                   jax.ShapeDtypeStruct((bsz, SSM_PAIRS, LANES, LANES), F32),
                   jax.ShapeDtypeStruct((bsz, SSM_CONV - 1, SSM_XBC), F32)],
        scratch_shapes=[pltpu.VMEM((1, SSM_INNER), F32)],
        compiler_params=_cparams(("parallel",), vmem),
        name="ssd_step",
    )(zxbc.reshape(bsz, 1, -1), dt_raw.reshape(bsz, 1, LANES),
      h_prev.astype(F32).reshape(bsz, SSM_PAIRS, LANES, LANES), conv_prev.astype(F32),
      w_conv.astype(F32), b_conv.reshape(1, SSM_XBC).astype(F32), dtb, aneg, dsk,
      g_out.reshape(1, SSM_INNER).astype(F32))
    return u.reshape(bsz, SSM_INNER), h_out.reshape(bsz, SSM_HEADS, SSM_HEAD_DIM, SSM_STATE), conv_out


def _ffn_up_kernel(a_ref, wg_ref, wu_ref, wc_ref, bc_ref, o_ref, st_ref, gp_ref, *, tiles_per_seq, tm):
    i = pl.program_id(1)

    @pl.when(i % tiles_per_seq == 0)
    def _():
        gp_ref[0:8, :] = jnp.zeros((8, gp_ref.shape[1]), F32)

    a = a_ref[...]
    gate = jnp.dot(a, wg_ref[...].astype(BF16), preferred_element_type=F32)
    up = jnp.dot(a, wu_ref[...].astype(BF16), preferred_element_type=F32)
    gp_ref[8:8 + tm, :] = gate
    conv = bc_ref[...] + wc_ref[2:3, :] * gate
    conv = conv + wc_ref[1:2, :] * gp_ref[7:7 + tm, :]
    conv = conv + wc_ref[0:1, :] * gp_ref[6:6 + tm, :]
    o_ref[...] = (_silu(conv) * up).astype(o_ref.dtype)
    st_ref[0] = gp_ref[tm + 6:tm + 8, :]
    gp_ref[0:8, :] = gp_ref[tm:tm + 8, :]


def ffn_up(h, bsz, t_len, w_up, w_conv, b_conv, layer, tm=1024, tn=256):
    m, d = h.shape
    d_ff = w_up.shape[2] // 2
    tm = min(tm, t_len)
    nj = d_ff // tn
    assert d_ff % tn == 0 and t_len % tm == 0 and tm % 8 == 0
    tps = t_len // tm
    vmem = 2 * tm * d * 2 + 4 * d * tn * 4 + 2 * d * tn * 2 + 2 * tm * tn * 2 + (tm + 8) * tn * 4 + 6 * tm * tn * 4
    kern = functools.partial(_ffn_up_kernel, tiles_per_seq=tps, tm=tm)
    return pl.pallas_call(
        kern,
        grid=(nj, m // tm),
        in_specs=[pl.BlockSpec((tm, d), lambda j, i: (i, 0)),
                  pl.BlockSpec((None, d, tn), lambda j, i: (layer, 0, j)),
                  pl.BlockSpec((None, d, tn), lambda j, i: (layer, 0, j + nj)),
                  pl.BlockSpec((None, FFN_CONV, tn), lambda j, i: (layer, 0, j)),
                  pl.BlockSpec((None, 1, tn), lambda j, i: (layer, 0, j))],
        out_specs=[pl.BlockSpec((tm, tn), lambda j, i: (i, j)),
                   pl.BlockSpec((1, FFN_CONV - 1, tn), lambda j, i: (i // tps, 0, j))],
        out_shape=[jax.ShapeDtypeStruct((m, d_ff), BF16),
                   jax.ShapeDtypeStruct((bsz, FFN_CONV - 1, d_ff), F32)],
        scratch_shapes=[pltpu.VMEM((tm + 8, tn), F32)],
        compiler_params=_cparams(("parallel", "arbitrary"), vmem),
        name="ffn_up",
    )(h, w_up, w_up, w_conv.astype(F32), b_conv.reshape(-1, 1, d_ff).astype(F32))


def _ffn_step_kernel(gu_ref, st_ref, wc_ref, bc_ref, o_ref, stout_ref, *, d_ff):
    gate = gu_ref[:, 0:d_ff]
    up = gu_ref[:, d_ff:2 * d_ff]
    conv = bc_ref[...] + wc_ref[2:3, :] * gate + wc_ref[1:2, :] * st_ref[1] + wc_ref[0:1, :] * st_ref[0]
    o_ref[...] = (_silu(conv) * up).astype(o_ref.dtype)
    stout_ref[0] = st_ref[1]
    stout_ref[1] = gate


def ffn_step(gu, state, w_conv, b_conv):
    bsz, two_dff = gu.shape
    d_ff = two_dff // 2
    a, st = pl.pallas_call(
        functools.partial(_ffn_step_kernel, d_ff=d_ff),
        out_shape=[jax.ShapeDtypeStruct((bsz, d_ff), BF16),
                   jax.ShapeDtypeStruct((FFN_CONV - 1, bsz, d_ff), F32)],
        compiler_params=_cparams((), 16 << 20),
        name="ffn_step",
    )(gu, jnp.swapaxes(state.astype(F32), 0, 1), w_conv.astype(F32), b_conv.reshape(1, d_ff).astype(F32))
    return a, jnp.swapaxes(st, 0, 1)


def _dil_attn_kernel(q0_ref, q1_ref, q2_ref, k0_ref, k1_ref, k2_ref, v0_ref, v1_ref, v2_ref,
                     o_ref, m_ref, l_ref, acc_ref, *, t_len):
    q_refs, k_refs, v_refs = (q0_ref, q1_ref, q2_ref), (k0_ref, k1_ref, k2_ref), (v0_ref, v1_ref, v2_ref)
    scale = HEAD_DIM ** -0.5
    blk = 128
    ii = lax.broadcasted_iota(jnp.int32, (blk, blk), 0)
    jj = lax.broadcasted_iota(jnp.int32, (blk, blk), 1)
    cur_ok = jj <= ii
    prev_ok = jj >= ii
    nt = (((1,), (1,)), ((), ()))
    for g, (w, r) in enumerate(DIL_PATTERNS):
        assert w // r == blk
        n_blk = t_len // (r * blk)
        for cls in range(r):
            for qb in range(n_blk):
                rows = pl.ds(cls + qb * blk * r, blk, stride=r) if r > 1 else pl.ds(qb * blk, blk)
                q = q_refs[g][rows, :].astype(BF16)
                k = k_refs[g][rows, :].astype(BF16)
                v = v_refs[g][rows, :].astype(BF16)
                s = lax.dot_general(q, k, nt, preferred_element_type=F32) * scale
                s = jnp.where(cur_ok, s, -jnp.inf)
                m = jnp.max(s, axis=-1, keepdims=True)
                if qb > 0:
                    prow = (pl.ds(cls + (qb - 1) * blk * r, blk, stride=r) if r > 1
                            else pl.ds((qb - 1) * blk, blk))
                    kp = k_refs[g][prow, :].astype(BF16)
                    vp = v_refs[g][prow, :].astype(BF16)
                    sp = lax.dot_general(q, kp, nt, preferred_element_type=F32) * scale
                    sp = jnp.where(prev_ok, sp, -jnp.inf)
                    m = jnp.maximum(m, jnp.max(sp, axis=-1, keepdims=True))
                p = jnp.exp(s - m)
                den = jnp.sum(p, axis=-1, keepdims=True)
                acc = jnp.dot(p.astype(BF16), v, preferred_element_type=F32)
                if qb > 0:
                    pp = jnp.exp(sp - m)
                    den = den + jnp.sum(pp, axis=-1, keepdims=True)
                    acc = acc + jnp.dot(pp.astype(BF16), vp, preferred_element_type=F32)
                m = jnp.broadcast_to(m, (blk, HEAD_DIM))
                den = jnp.broadcast_to(den, (blk, HEAD_DIM))
                if g == 0:
                    m_ref[rows, :] = m
                    l_ref[rows, :] = den
                    acc_ref[rows, :] = acc
                else:
                    m_old = m_ref[rows, :]
                    m_new = jnp.maximum(m_old, m)
                    a_old = jnp.exp(m_old - m_new)
                    a_new = jnp.exp(m - m_new)
                    m_ref[rows, :] = m_new
                    l_ref[rows, :] = l_ref[rows, :] * a_old + den * a_new
                    acc_ref[rows, :] = acc_ref[rows, :] * a_old + acc * a_new
    o_ref[...] = (acc_ref[...] / l_ref[...]).astype(o_ref.dtype)


def dilated_attention_prompt(qq, kv, bsz, t_len):
    nh = DIL_HEADS
    tspec = lambda colfn: pl.BlockSpec((t_len, HEAD_DIM), lambda b, h: (b, colfn(h)))
    in_specs = ([tspec(lambda h, g=g: g * nh + h) for g in range(3)]
                + [tspec(lambda h, g=g: g * 2 * nh + h) for g in range(3)]
                + [tspec(lambda h, g=g: g * 2 * nh + nh + h) for g in range(3)])
    vmem = 2 * 9 * t_len * HEAD_DIM * 4 + 2 * t_len * HEAD_DIM * 2 + 3 * t_len * HEAD_DIM * 4 + (8 << 20)
    return pl.pallas_call(
        functools.partial(_dil_attn_kernel, t_len=t_len),
        grid=(bsz, nh),
        in_specs=in_specs,
        out_specs=pl.BlockSpec((t_len, HEAD_DIM), lambda b, h: (b, h)),
        out_shape=jax.ShapeDtypeStruct((bsz * t_len, nh * HEAD_DIM), BF16),
        scratch_shapes=[pltpu.VMEM((t_len, HEAD_DIM), F32)] * 3,
        compiler_params=_cparams(("parallel", "parallel"), vmem),
        name="dilated_attention",
    )(qq, qq, qq, kv, kv, kv, kv, kv, kv)


def _dil_step_kernel(q_ref, kvn_ref, c0_ref, c1_ref, c2_ref, o_ref):
    caches = (c0_ref, c1_ref, c2_ref)
    scale = HEAD_DIM ** -0.5
    nh = DIL_HEADS
    s_past, s_new = [], []
    m = None
    for g in range(3):
        q = q_ref[0, g]
        s = jnp.sum(caches[g][0, :, 0:nh, :] * q[None], axis=-1, keepdims=True) * scale
        sn = jnp.sum(kvn_ref[0, g, 0] * q, axis=-1, keepdims=True) * scale
        s_past.append(s)
        s_new.append(sn)
        mg = jnp.maximum(jnp.max(s, axis=0), sn)
        m = mg if m is None else jnp.maximum(m, mg)
    den = jnp.zeros((nh, 1), F32)
    acc = jnp.zeros((nh, HEAD_DIM), F32)
    for g in range(3):
        p = jnp.exp(s_past[g] - m[None])
        pn = jnp.exp(s_new[g] - m)
        den = den + jnp.sum(p, axis=0) + pn
        acc = acc + jnp.sum(p * caches[g][0, :, nh:2 * nh, :], axis=0) + pn * kvn_ref[0, g, 1]
    o_ref[0] = (acc / den).astype(o_ref.dtype)


def dilated_attention_step(q, kv_new, caches):
    bsz = q.shape[0]
    nh = DIL_HEADS
    views = []
    for cache, (w, r) in zip(caches, DIL_PATTERNS):
        assert cache.shape[1] == w, "rolling window cache must hold the full window"
        views.append(cache.reshape(bsz, w // r, r * 2 * nh, HEAD_DIM))
    nkeys = DIL_PATTERNS[0][0] // DIL_PATTERNS[0][1]
    return pl.pallas_call(
        _dil_step_kernel,
        grid=(bsz,),
        in_specs=[pl.BlockSpec((1, 3, nh, HEAD_DIM), lambda b: (b, 0, 0, 0)),
                  pl.BlockSpec((1, 3, 2, nh, HEAD_DIM), lambda b: (b, 0, 0, 0, 0))]
                 + [pl.BlockSpec((1, nkeys, 2 * nh, HEAD_DIM), lambda b: (b, 0, 0, 0)) for _ in DIL_PATTERNS],
        out_specs=pl.BlockSpec((1, nh, HEAD_DIM), lambda b: (b, 0, 0)),
        out_shape=jax.ShapeDtypeStruct((bsz, nh, HEAD_DIM), F32),
        compiler_params=_cparams(("parallel",), 2 * 3 * nkeys * 2 * nh * HEAD_DIM * 4 + (8 << 20)),
        name="dilated_attention_step",
    )(q, kv_new, *views)


def _mem_step_kernel(q_ref, kv_ref, o_ref):
    q = q_ref[0]
    s = jnp.sum(kv_ref[0, :, 0] * q[None], axis=-1, keepdims=True) * (MEM_HEAD_DIM ** -0.5)
    p = jnp.exp(s - jnp.max(s, axis=0)[None])
    o_ref[0] = (jnp.sum(p * kv_ref[0, :, 1], axis=0) / jnp.sum(p, axis=0)).astype(o_ref.dtype)


def mem_attention_step(q, cache, layer):
    bsz = q.shape[0]
    return pl.pallas_call(
        _mem_step_kernel,
        grid=(bsz,),
        in_specs=[pl.BlockSpec((1, MEM_HEADS, MEM_HEAD_DIM), lambda b: (b, 0, 0)),
                  pl.BlockSpec((None, 1, N_MEM, 2, MEM_HEADS, MEM_HEAD_DIM), lambda b: (layer, b, 0, 0, 0, 0))],
        out_specs=pl.BlockSpec((1, MEM_HEADS, MEM_HEAD_DIM), lambda b: (b, 0, 0)),
        out_shape=jax.ShapeDtypeStruct((bsz, MEM_HEADS, MEM_HEAD_DIM), F32),
        compiler_params=_cparams(("parallel",), 4 * N_MEM * 2 * 8 * MEM_HEAD_DIM * 4 + (8 << 20)),
        name="mem_attention_step",
    )(q, cache)


def _trunk(x, bsz, t_len, pos, mem_kv, ssm_prev, conv_prev, ffn_prev, win_past, p):
    m, d = x.shape
    prompt = ssm_prev is None
    rope = rope_tables(pos) if prompt else tuple(jnp.broadcast_to(t, (m, HEAD_DIM)) for t in rope_tables(pos))
    nh = DIL_HEADS
    ffn_states = []

    def conv_ffn(x, i):
        h = rmsnorm_rows(x, p["g_ffn"][i])
        if prompt:
            a, st = ffn_up(h, bsz, t_len, p["w_ffn_up"], p["w_ffn_conv"], p["b_ffn_conv"], i)
        else:
            gu = matmul([h], p["w_ffn_up"], i, n=p["w_ffn_up"].shape[2], name="ffn_up_raw")
            a, st = ffn_step(gu, ffn_prev[i], p["w_ffn_conv"][i], p["b_ffn_conv"][i])
        ffn_states.append(st)
        return matmul_long_k(a, p["w_ffn_down"], i, x, name="ffn_down")

    def memory_attention(q, q_colblk, layer):
        if prompt:
            return mem_attention(q.reshape(bsz, t_len, -1), q_colblk, mem_kv[layer]).reshape(m, MEM_W)
        q = q[:, q_colblk * MEM_W:(q_colblk + 1) * MEM_W].reshape(m, MEM_HEADS, MEM_HEAD_DIM)
        return mem_attention_step(q, mem_kv, layer).reshape(m, MEM_W)

    w_in = p["w_in_a"]
    zx_cols = SSM_INNER + SSM_XBC
    h = rmsnorm_rows(x, p["g_mix"][0])
    zxbc = matmul([h], w_in, 0, n=zx_cols, name="in_proj_a")
    dt_raw = matmul([h], w_in, 0, n=LANES, w_col0=zx_cols, tn=LANES, name="in_proj_dt")
    qm = matmul([h], w_in[:, :, zx_cols + SSM_HEADS:], 0, n=MEM_W, modes=["norm256"] * 2,
                gain=jnp.tile(p["g_mem_q"][0], MEM_HEADS), name="in_proj_qmem")
    y_mem = memory_attention(qm, 0, 0)
    ssm_args = (p["w_conv_a"][0], p["b_conv_a"][0], p["dt_bias_a"][0], p["a_log_a"][0], p["d_skip_a"][0],
                p["g_ssm_out_a"][0])
    if prompt:
        u, ssm_new = ssd_mixer(zxbc, dt_raw, bsz, t_len, *ssm_args)
        conv_new = zxbc.reshape(bsz, t_len, zx_cols)[:, t_len - (SSM_CONV - 1):, SSM_INNER:]
    else:
        u, ssm_new, conv_new = ssd_step(zxbc, dt_raw, ssm_prev[0], conv_prev[0], *ssm_args)
    x = matmul([u, y_mem], p["w_out_a"], 0, n=d, res=x, name="out_proj_a")
    x = conv_ffn(x, 0)

    gk = jnp.concatenate([jnp.concatenate([jnp.tile(p["g_k_dil"][g], nh), jnp.ones((nh * HEAD_DIM,), F32)])
                          for g in range(3)])
    hk = rmsnorm_rows(x, p["g_kv"])
    kv = matmul([hk], p["w_kv"][None], 0, n=6 * nh * HEAD_DIM, modes=(["norm128rope"] * 2 + ["plain"] * 2) * 3,
                gain=gk, rope=rope, name="kv_proj")

    gq = jnp.concatenate([jnp.tile(p["g_q_dil"][0][g], nh) for g in range(3)]
                         + [jnp.tile(p["g_mem_q"][1], MEM_HEADS)])
    h = rmsnorm_rows(x, p["g_mix"][1])
    qq = matmul([h], p["w_in_b"], 0, n=d, modes=["norm128rope"] * 6 + ["norm256"] * 2, gain=gq, rope=rope,
                name="in_proj_b")
    y_mem = memory_attention(qq, 3, 1)
    if prompt:
        att = dilated_attention_prompt(qq, kv, bsz, t_len)
    else:
        att = dilated_attention_step(qq[:, :3 * nh * HEAD_DIM].reshape(m, 3, nh, HEAD_DIM),
                                     kv.reshape(m, 3, 2, nh, HEAD_DIM), win_past).reshape(m, nh * HEAD_DIM)
    x = matmul([att, y_mem], p["w_out_b"], 0, n=d, res=x, name="out_proj_b")
    x = conv_ffn(x, 1)
    return x, ssm_new, conv_new, jnp.stack(ffn_states, axis=0), kv


def _memory_kv(mem, g_norm, w_kv, layer, g_k):
    bsz, n, d = mem.shape
    hm = rmsnorm_rows(mem.reshape(bsz * n, d), g_norm)
    gain = jnp.concatenate([jnp.tile(g_k, MEM_HEADS), jnp.ones((MEM_W,), F32)])
    kv = matmul([hm], w_kv, layer, n=2 * MEM_W, modes=["norm256"] * 2 + ["plain"] * 2, gain=gain,
                name="mem_kv_proj")
    return kv.reshape(bsz, n, 2 * MEM_W)


def kernel(x_prompt, x_sample, state_ssm, state_ssm_conv, state_ffn_conv, cache_mem_kv, cache_win_kv0, cache_win_kv1, cache_win_kv2, mem_prompt, g_mix, w_in_a, w_conv_a, b_conv_a, dt_bias_a, a_log_a, d_skip_a, g_ssm_out_a, w_out_a, g_kv, w_kv, g_k_dil, w_in_b, g_q_dil, w_out_b, g_mem, w_mem_kv, g_mem_q, g_mem_k, g_ffn, w_ffn_up, w_ffn_conv, b_ffn_conv, w_ffn_down):
    p = dict(g_mix=g_mix, w_in_a=w_in_a, w_conv_a=w_conv_a, b_conv_a=b_conv_a, dt_bias_a=dt_bias_a,
             a_log_a=a_log_a, d_skip_a=d_skip_a, g_ssm_out_a=g_ssm_out_a, w_out_a=w_out_a, g_kv=g_kv, w_kv=w_kv,
             g_k_dil=g_k_dil, w_in_b=w_in_b, g_q_dil=g_q_dil, w_out_b=w_out_b, g_mem_q=g_mem_q, g_ffn=g_ffn,
             w_ffn_up=w_ffn_up, w_ffn_conv=w_ffn_conv, b_ffn_conv=b_ffn_conv, w_ffn_down=w_ffn_down)
    bp, sp, d = x_prompt.shape
    bs, ds, _ = x_sample.shape
    depth = g_mix.shape[0]
    nh = DIL_HEADS

    mem_kv_p = [_memory_kv(mem_prompt, g_mem[i], w_mem_kv, i, g_mem_k[i]) for i in range(depth)]
    y_p, ssm_p, conv_p, ffn_p, kv_p = _trunk(x_prompt.reshape(bp * sp, d), bp, sp, jnp.arange(sp, dtype=jnp.int32),
                                             mem_kv_p, None, None, None, None, p)
    assert ds == 1
    pos_s = PAST_LEN + jnp.arange(ds, dtype=jnp.int32)
    y_s, ssm_s, conv_s, ffn_s, kv_s = _trunk(x_sample.reshape(bs * ds, d), bs, ds, pos_s, cache_mem_kv, state_ssm,
                                             state_ssm_conv, state_ffn_conv,
                                             [cache_win_kv0, cache_win_kv1, cache_win_kv2], p)

    kv_p = kv_p.reshape(bp, sp, 3, 2, nh, HEAD_DIM)
    kv_s = kv_s.reshape(bs, ds, 3, 2, nh, HEAD_DIM)
    win_p = [kv_p[:, sp - min(w, sp):, g] for g, (w, _) in enumerate(DIL_PATTERNS)]
    mem_out = jnp.stack(mem_kv_p, axis=0).reshape(depth, bp, N_MEM, 2, MEM_HEADS, MEM_HEAD_DIM)
    return (y_p.reshape(bp, sp, d), y_s.reshape(bs, ds, d), ssm_p[None], ssm_s[None], conv_p[None], conv_s[None],
            ffn_p, ffn_s, mem_out, win_p[0], win_p[1], win_p[2], kv_s[:, :, 0], kv_s[:, :, 1], kv_s[:, :, 2])
```

```python
import functools
import math

import jax
import jax.numpy as jnp
from jax import lax
from jax.experimental import pallas as pl
from jax.experimental.pallas import tpu as pltpu

F32 = jnp.float32
BF16 = jnp.bfloat16
EPS = 1e-6
LANES = 128
V7X_VMEM_BYTES = 64 * 2**20

SSM_HEAD_DIM = 64
SSM_HEADS = 48
SSM_GROUPS = 8
SSM_STATE = 128
SSM_INNER = SSM_HEADS * SSM_HEAD_DIM
SSM_BC = SSM_GROUPS * SSM_STATE
SSM_XBC = SSM_INNER + 2 * SSM_BC
SSM_CONV = 4
SSM_CHUNK = 128
SSM_PAIRS = SSM_HEADS // 2
DIL_PATTERNS = ((128, 1), (512, 4), (2048, 16))
DIL_HEADS = 8
HEAD_DIM = 128
ROT_DIM = HEAD_DIM // 4
ROPE_THETA = 500000.0
N_MEM = 256
MEM_HEADS = 4
MEM_HEAD_DIM = 256
MEM_W = MEM_HEADS * MEM_HEAD_DIM
FFN_CONV = 3
PAST_LEN = 16384


def _cparams(sem, vmem_bytes):
    limit = min(int(vmem_bytes * 1.25) + (4 << 20), V7X_VMEM_BYTES - (6 << 20))
    return pltpu.CompilerParams(dimension_semantics=sem or None, vmem_limit_bytes=limit)


def _silu(x):
    return x * jax.nn.sigmoid(x)


def _softplus(x):
    return jnp.maximum(x, 0.0) + jnp.log1p(jnp.exp(-jnp.abs(x)))


def _rmsnorm_kernel(x_ref, g_ref, o_ref):
    x = x_ref[...]
    ms = jnp.mean(x * x, axis=-1, keepdims=True)
    o_ref[...] = (x * lax.rsqrt(ms + EPS) * g_ref[...]).astype(o_ref.dtype)


def rmsnorm_rows(x, g, tm=256):
    m, d = x.shape
    tm = min(tm, m)
    return pl.pallas_call(
        _rmsnorm_kernel,
        grid=(m // tm,),
        in_specs=[pl.BlockSpec((tm, d), lambda i: (i, 0)),
                  pl.BlockSpec((1, d), lambda i: (0, 0))],
        out_specs=pl.BlockSpec((tm, d), lambda i: (i, 0)),
        out_shape=jax.ShapeDtypeStruct((m, d), BF16),
        compiler_params=_cparams(("parallel",), 2 * tm * d * 6),
        name="rmsnorm_rows",
    )(x, g.reshape(1, d).astype(F32))


def rope_tables(pos):
    half = ROT_DIM // 2
    inv_freq = jnp.exp(-(2.0 * jnp.arange(half, dtype=F32) / ROT_DIM) * math.log(ROPE_THETA))
    ang = pos.astype(F32)[:, None] * inv_freq[None, :]
    cos, sin = jnp.cos(ang), jnp.sin(ang)
    n = pos.shape[0]
    ones = jnp.ones((n, HEAD_DIM - ROT_DIM), F32)
    zeros = jnp.zeros((n, HEAD_DIM - ROT_DIM), F32)
    zh = jnp.zeros((n, half), F32)
    c = jnp.concatenate([cos, cos, ones], axis=1)
    s_lo = jnp.concatenate([-sin, zh, zeros], axis=1)
    s_hi = jnp.concatenate([zh, sin, zeros], axis=1)
    return c, s_lo, s_hi


def _head_norm(blk, gain, hd):
    ms = jnp.sum(blk * blk, axis=-1, keepdims=True) * (1.0 / hd)
    return blk * lax.rsqrt(ms + EPS) * gain


def _row_scale(ssq_ref, d_model):
    return lax.rsqrt(jnp.sum(ssq_ref[...], axis=-1, keepdims=True) * (1.0 / d_model) + EPS)


def _lane_partial_ssq(y):
    y2 = y * y
    part = y2[:, 0:LANES]
    for c0 in range(LANES, y.shape[1], LANES):
        part = part + y2[:, c0:c0 + LANES]
    return part


def _mm_kernel(*refs, n_a, modes, nj, n_tiles, tn, w_t, has_gain, has_rope, has_res, d_scale, n_emit,
               pipelined):
    it = iter(refs)
    a_refs = [next(it) for _ in range(n_a)]
    w_refs = [next(it) for _ in range(n_a)]
    ssq_in_ref = next(it) if d_scale else None
    gain_ref = next(it) if has_gain else None
    rope_refs = (next(it), next(it), next(it)) if has_rope else None
    res_ref = next(it) if has_res else None
    emit_gain_refs = [next(it) for _ in range(n_emit)]
    o_ref = next(it)
    emit_refs = [next(it) for _ in range(n_emit)]
    ssq_out_ref = next(it) if n_emit else None
    p_bufs = (next(it), next(it)) if pipelined else None
    s = pl.program_id(0)
    j_epi = (jnp.maximum(s - 1, 0) if pipelined else s) % nj

    def product():
        p = None
        for a_ref, w_ref in zip(a_refs, w_refs):
            wb = w_ref[...].astype(BF16)
            dims = (((1,), (1,)), ((), ())) if w_t else (((1,), (0,)), ((), ()))
            d = lax.dot_general(a_ref[...].astype(BF16), wb, dims, preferred_element_type=F32)
            p = d if p is None else p + d
        return p

    def epilogue(mode, p):
        if d_scale:
            p = p * _row_scale(ssq_in_ref, d_scale)
        if mode == "plain":
            y = p + res_ref[...] if has_res else p
            o_ref[...] = y.astype(o_ref.dtype)
            if n_emit:
                for g_ref, e_ref in zip(emit_gain_refs, emit_refs):
                    e_ref[...] = (y * g_ref[...]).astype(e_ref.dtype)
                part = _lane_partial_ssq(y)

                @pl.when(j_epi == 0)
                def _():
                    ssq_out_ref[...] = part

                @pl.when(j_epi > 0)
                def _():
                    ssq_out_ref[...] += part
            return
        hd = 256 if mode == "norm256" else 128
        for h0 in range(0, tn, hd):
            y = _head_norm(p[:, h0:h0 + hd], gain_ref[:, h0:h0 + hd], hd)
            if mode == "norm128rope":
                c_ref, slo_ref, shi_ref = rope_refs
                half = ROT_DIM // 2
                y = (y * c_ref[...]
                     + pltpu.roll(y, hd - half, 1) * slo_ref[...]
                     + pltpu.roll(y, half, 1) * shi_ref[...])
            o_ref[:, h0:h0 + hd] = y.astype(o_ref.dtype)

    def stage(mode, parity):
        if pipelined:
            new, old = (p_bufs[0], p_bufs[1]) if parity == 0 else (p_bufs[1], p_bufs[0])
            new[...] = product()
            epilogue(mode, old[...])
        else:
            epilogue(mode, product())

    if pipelined:
        @pl.when(s == 0)
        def _():
            p_bufs[1][...] = jnp.zeros_like(p_bufs[1])

    distinct = sorted(set(modes))
    for mode in distinct:
        cond = None
        if len(distinct) > 1:
            for jj, mj in enumerate(modes):
                if mj == mode:
                    c = j_epi == jj
                    cond = c if cond is None else jnp.logical_or(cond, c)
        for parity in ((0, 1) if pipelined else (None,)):
            c = cond
            if parity is not None:
                cp = (s % 2) == parity
                c = cp if c is None else jnp.logical_and(c, cp)
            if c is None:
                stage(mode, parity)
            else:
                pl.when(c)(functools.partial(stage, mode, parity))


def matmul(a_list, w, layer, *, n, tm=1024, tn=512, w_col0=0, w_t=False, out_dtype=F32, modes=None,
           gain=None, rope=None, res=None, row_ssq=None, emit_gains=(), name="matmul"):
    m = a_list[0].shape[0]
    tm = min(tm, m)
    assert m % tm == 0 and n % tn == 0 and w_col0 % tn == 0
    nj = n // tn
    n_tiles = (m // tm) * nj
    pipelined = n_tiles >= 16
    modes = tuple(modes) if modes is not None else ("plain",) * nj
    assert len(modes) == nj
    cb = w_col0 // tn
    cur = (lambda s: jnp.minimum(s, n_tiles - 1)) if pipelined else (lambda s: s)
    epi = (lambda s: jnp.maximum(s - 1, 0)) if pipelined else (lambda s: s)
    in_specs, w_specs, vmem, row0 = [], [], 0, 0
    for a in a_list:
        ki = a.shape[1]
        assert row0 % ki == 0
        rb = row0 // ki
        in_specs.append(pl.BlockSpec((tm, ki), lambda s: (cur(s) // nj, 0), pipeline_mode=pl.Buffered(1)))
        if w_t:
            w_specs.append(pl.BlockSpec((None, tn, ki), lambda s, rb=rb: (layer, cur(s) % nj + cb, rb)))
        else:
            w_specs.append(pl.BlockSpec((None, ki, tn), lambda s, rb=rb: (layer, rb, cur(s) % nj + cb)))
        vmem += tm * ki * a.dtype.itemsize + tm * ki * 2 + 2 * ki * tn * 4 + ki * tn * 2
        row0 += ki
    assert row0 == w.shape[2 if w_t else 1]
    in_specs += w_specs
    args = list(a_list) + [w] * len(a_list)
    vmem += 2 * tm * tn * jnp.dtype(out_dtype).itemsize + 4 * tm * tn * 4
    tile_spec = pl.BlockSpec((tm, tn), lambda s: (epi(s) // nj, epi(s) % nj))
    col_spec = pl.BlockSpec((1, tn), lambda s: (0, epi(s) % nj))
    ssq_spec = pl.BlockSpec((tm, LANES), lambda s: (epi(s) // nj, 0))
    if row_ssq is not None:
        in_specs.append(ssq_spec)
        args.append(row_ssq[0])
    if gain is not None:
        in_specs.append(col_spec)
        args.append(gain.reshape(1, n).astype(F32))
    if rope is not None:
        period = rope[0].shape[0]
        assert period % tm == 0
        nper = period // tm
        for t in rope:
            in_specs.append(pl.BlockSpec((tm, HEAD_DIM), lambda s: ((epi(s) // nj) % nper, 0)))
            args.append(t)
    if res is not None:
        in_specs.append(tile_spec)
        args.append(res)
        vmem += 2 * tm * tn * 4
    out_specs = [tile_spec]
    out_shape = [jax.ShapeDtypeStruct((m, n), out_dtype)]
    if emit_gains:
        assert nj * tn == n and all(g.shape == (n,) for g in emit_gains)
        for g in emit_gains:
            in_specs.append(col_spec)
            args.append(g.reshape(1, n).astype(F32))
            out_specs.append(tile_spec)
            out_shape.append(jax.ShapeDtypeStruct((m, n), BF16))
            vmem += 2 * tm * tn * 2
        out_specs.append(ssq_spec)
        out_shape.append(jax.ShapeDtypeStruct((m, LANES), F32))
    kern = functools.partial(_mm_kernel, n_a=len(a_list), modes=modes, nj=nj, n_tiles=n_tiles, tn=tn, w_t=w_t,
                             has_gain=gain is not None, has_rope=rope is not None, has_res=res is not None,
                             d_scale=row_ssq[1] if row_ssq is not None else 0, n_emit=len(emit_gains),
                             pipelined=pipelined)
    outs = pl.pallas_call(
        kern,
        grid=(n_tiles + 1 if pipelined else n_tiles,),
        in_specs=in_specs,
        out_specs=out_specs,
        out_shape=out_shape,
        scratch_shapes=[pltpu.VMEM((tm, tn), F32)] * 2 if pipelined else [],
        compiler_params=_cparams(("arbitrary",), vmem),
        name=name,
    )(*args)
    return outs if emit_gains else outs[0]


def _mm_panel_kernel(*refs, nk, n_emit):
    a_ref, w_ref, res_ref = refs[0:3]
    emit_gain_refs = refs[3:3 + n_emit]
    o_ref = refs[3 + n_emit]
    emit_refs = refs[4 + n_emit:4 + 2 * n_emit]
    ssq_out_ref = refs[4 + 2 * n_emit] if n_emit else None
    acc_ref = refs[-1]
    k, j = pl.program_id(1), pl.program_id(2)
    p = jnp.dot(a_ref[...].astype(BF16), w_ref[...].astype(BF16), preferred_element_type=F32)

    @pl.when(k == 0)
    def _():
        acc_ref[j] = p

    if nk > 2:
        @pl.when(jnp.logical_and(k > 0, k < nk - 1))
        def _():
            acc_ref[j] += p

    @pl.when(k == nk - 1)
    def _():
        y = acc_ref[j] + p + res_ref[...]
        o_ref[...] = y.astype(o_ref.dtype)
        if n_emit:
            for g_ref, e_ref in zip(emit_gain_refs, emit_refs):
                e_ref[...] = (y * g_ref[...]).astype(e_ref.dtype)
            part = _lane_partial_ssq(y)

            @pl.when(j == 0)
            def _():
                ssq_out_ref[...] = part

            @pl.when(j > 0)
            def _():
                ssq_out_ref[...] += part


def matmul_long_k(a, w, layer, res, *, tm=1024, tn=256, nk=2, emit_gains=(), name="matmul_long_k"):
    m, kdim = a.shape
    n = w.shape[2]
    tm = min(tm, m)
    assert kdim % nk == 0 and (kdim // nk) % LANES == 0 and m % tm == 0 and n % tn == 0 and nk >= 2
    tk = kdim // nk
    nj = n // tn
    last = lambda k, j: jnp.where(k == nk - 1, j, 0)
    tile_spec = pl.BlockSpec((tm, tn), lambda i, k, j: (i, last(k, j)))
    col_spec = pl.BlockSpec((1, tn), lambda i, k, j: (0, last(k, j)))
    vmem = tm * tk * 2 + 2 * tk * tn * 4 + tk * tn * 2 + nj * tm * tn * 4 + 5 * tm * tn * 4
    in_specs = [pl.BlockSpec((tm, tk), lambda i, k, j: (i, k), pipeline_mode=pl.Buffered(1)),
                pl.BlockSpec((None, tk, tn), lambda i, k, j: (layer, k, j)),
                tile_spec]
    args = [a, w, res]
    out_specs = [tile_spec]
    out_shape = [jax.ShapeDtypeStruct((m, n), F32)]
    for g in emit_gains:
        in_specs.append(col_spec)
        args.append(g.reshape(1, n).astype(F32))
        out_specs.append(tile_spec)
        out_shape.append(jax.ShapeDtypeStruct((m, n), BF16))
        vmem += 2 * tm * tn * 2
    if emit_gains:
        out_specs.append(pl.BlockSpec((tm, LANES), lambda i, k, j: (i, 0)))
        out_shape.append(jax.ShapeDtypeStruct((m, LANES), F32))
    outs = pl.pallas_call(
        functools.partial(_mm_panel_kernel, nk=nk, n_emit=len(emit_gains)),
        grid=(m // tm, nk, nj),
        in_specs=in_specs,
        out_specs=out_specs,
        out_shape=out_shape,
        scratch_shapes=[pltpu.VMEM((nj, tm, tn), F32)],
        compiler_params=_cparams(("parallel", "arbitrary", "arbitrary"), vmem),
        name=name,
    )(*args)
    return outs if emit_gains else outs[0]


def _mem_attn_kernel(q_ref, kv_ref, o_ref):
    scale = MEM_HEAD_DIM ** -0.5
    for h in range(MEM_HEADS):
        lo = h * MEM_HEAD_DIM
        q = q_ref[0, :, lo:lo + MEM_HEAD_DIM].astype(BF16)
        tq = q.shape[0]
        if tq < 8:
            q = jnp.broadcast_to(q, (8, MEM_HEAD_DIM))
        k = kv_ref[0, :, lo:lo + MEM_HEAD_DIM].astype(BF16)
        v = kv_ref[0, :, MEM_W + lo:MEM_W + lo + MEM_HEAD_DIM].astype(BF16)
        s = lax.dot_general(q, k, (((1,), (1,)), ((), ())), preferred_element_type=F32) * scale
        p = jnp.exp(s - jnp.max(s, axis=-1, keepdims=True))
        den = jnp.sum(p, axis=-1, keepdims=True)
        o = jnp.dot(p.astype(BF16), v, preferred_element_type=F32) / den
        o_ref[0, :, lo:lo + MEM_HEAD_DIM] = o[0:tq].astype(o_ref.dtype)


def mem_attention(q, q_colblk, mem_kv, tq=512):
    b, t, _ = q.shape
    tq = min(tq, t)
    return pl.pallas_call(
        _mem_attn_kernel,
        grid=(b, t // tq),
        in_specs=[pl.BlockSpec((1, tq, MEM_W), lambda i, j: (i, j, q_colblk)),
                  pl.BlockSpec((1, N_MEM, 2 * MEM_W), lambda i, j: (i, 0, 0))],
        out_specs=pl.BlockSpec((1, tq, MEM_W), lambda i, j: (i, j, 0)),
        out_shape=jax.ShapeDtypeStruct((b, t, MEM_W), BF16),
        compiler_params=_cparams(("parallel", "parallel"), 2 * (tq * MEM_W * 6 + N_MEM * 2 * MEM_W * 4) + 8 * tq * N_MEM * 4),
        name="mem_attention",
    )(q, mem_kv)


def _split3_dot(lhs_bf16, x):
    hi = x.astype(BF16)
    r1 = x - hi.astype(F32)
    mid = r1.astype(BF16)
    lo = (r1 - mid.astype(F32)).astype(BF16)
    out = jnp.dot(lhs_bf16, hi, preferred_element_type=F32)
    out += jnp.dot(lhs_bf16, mid, preferred_element_type=F32)
    out += jnp.dot(lhs_bf16, lo, preferred_element_type=F32)
    return out


def _ssd_kernel(z_ref, x_ref, b_ref, c_ref, dt_ref, wconv_ref, bconv_ref, dtb_ref, aneg_ref,
                dskip_ref, gout_ref, u_ref, hout_ref, xp_ref, xs_ref, h_ref, y_ref):
    ck = pl.program_id(1)
    n_ck = pl.num_programs(1)
    c = SSM_CHUNK

    @pl.when(ck == 0)
    def _():
        xp_ref[0:8, :] = jnp.zeros((8, SSM_XBC), F32)
        h_ref[...] = jnp.zeros_like(h_ref)

    xp_ref[8:8 + c, 0:SSM_INNER] = x_ref[...]
    xp_ref[8:8 + c, SSM_INNER:SSM_INNER + SSM_BC] = b_ref[...]
    xp_ref[8:8 + c, SSM_INNER + SSM_BC:SSM_XBC] = c_ref[...]

    slab = 512
    for c0 in range(0, SSM_XBC, slab):
        acc = bconv_ref[:, c0:c0 + slab] + wconv_ref[3:4, c0:c0 + slab] * xp_ref[8:8 + c, c0:c0 + slab]
        for tap in range(SSM_CONV - 1):
            off = 8 - (SSM_CONV - 1 - tap)
            acc = acc + wconv_ref[tap:tap + 1, c0:c0 + slab] * xp_ref[off:off + c, c0:c0 + slab]
        xs_ref[:, c0:c0 + slab] = _silu(acc)
    xp_ref[0:8, :] = xp_ref[c:c + 8, :]

    lane = lax.broadcasted_iota(jnp.int32, (c, LANES), 1)
    row = lax.broadcasted_iota(jnp.int32, (c, LANES), 0)
    dt = jnp.where(lane < SSM_HEADS, _softplus(dt_ref[...] + dtb_ref[...]), 0.0)
    la = dt * aneg_ref[...]
    tri = row >= lane
    cum = _split3_dot(jnp.where(tri, 1.0, 0.0).astype(BF16), la)
    cum_t = cum.T
    ecum = jnp.exp(cum)
    wend = jnp.exp(cum[c - 1:c, :] - cum)
    elast_t = jnp.exp(cum_t[:, c - 1:c])
    left = lane < SSM_HEAD_DIM
    top = row < SSM_HEAD_DIM

    def pair(col0, col1):
        return jnp.where(left, col0, col1)

    for g in range(SSM_GROUPS):
        bg = xs_ref[:, SSM_INNER + g * SSM_STATE:SSM_INNER + (g + 1) * SSM_STATE].astype(BF16)
        cg = xs_ref[:, SSM_INNER + SSM_BC + g * SSM_STATE:SSM_INNER + SSM_BC + (g + 1) * SSM_STATE].astype(BF16)
        cb = lax.dot_general(cg, bg, (((1,), (1,)), ((), ())), preferred_element_type=F32)
        for q in range(SSM_PAIRS // SSM_GROUPS):
            pr = g * (SSM_PAIRS // SSM_GROUPS) + q
            h0, h1 = 2 * pr, 2 * pr + 1
            xpair = xs_ref[:, pr * LANES:(pr + 1) * LANES]
            xdt = xpair * pair(dt[:, h0:h0 + 1], dt[:, h1:h1 + 1])
            xdt_b = xdt.astype(BF16)
            ys = []
            for hh in (h0, h1):
                seg = cum[:, hh:hh + 1] - cum_t[hh:hh + 1, :]
                decay = jnp.exp(jnp.where(tri, seg, -jnp.inf))
                ys.append(jnp.dot((cb * decay).astype(BF16), xdt_b, preferred_element_type=F32))
            y = pair(ys[0], ys[1])
            hp = h_ref[pr]
            y_in = lax.dot_general(cg, hp.astype(BF16), (((1,), (1,)), ((), ())), preferred_element_type=F32)
            y = y + y_in * pair(ecum[:, h0:h0 + 1], ecum[:, h1:h1 + 1])
            xw = xdt * pair(wend[:, h0:h0 + 1], wend[:, h1:h1 + 1])
            upd = jnp.dot(xw.T.astype(BF16), bg, preferred_element_type=F32)
            keep = jnp.where(top, elast_t[h0:h0 + 1, :], elast_t[h1:h1 + 1, :])
            h_ref[pr] = hp * keep + upd
            y_ref[:, pr * LANES:(pr + 1) * LANES] = y + dskip_ref[:, pr * LANES:(pr + 1) * LANES] * xpair

    gw = SSM_INNER // SSM_GROUPS
    for g in range(SSM_GROUPS):
        sl = slice(g * gw, (g + 1) * gw)
        u = y_ref[:, sl] * _silu(z_ref[:, sl])
        ms = jnp.sum(u * u, axis=-1, keepdims=True) * (1.0 / gw)
        u_ref[:, sl] = (u * lax.rsqrt(ms + EPS) * gout_ref[:, sl]).astype(u_ref.dtype)

    @pl.when(ck == n_ck - 1)
    def _():
        hout_ref[0] = h_ref[...]


def ssd_mixer(zxbc, dt_raw, bsz, t_len, w_conv, b_conv, dt_bias, a_log, d_skip, g_out):
    c = SSM_CHUNK
    n_ck = t_len // c
    xblk = SSM_INNER // SSM_BC
    pad = LANES - SSM_HEADS
    dtb = jnp.pad(dt_bias.astype(F32), (0, pad)).reshape(1, LANES)
    aneg = jnp.pad(-jnp.exp(a_log.astype(F32)), (0, pad)).reshape(1, LANES)
    dsk = jnp.repeat(d_skip.astype(F32), SSM_HEAD_DIM).reshape(1, SSM_INNER)
    row_spec = lambda width, blk: pl.BlockSpec((c, width), lambda i, j: (i * n_ck + j, blk))
    const = lambda shape: pl.BlockSpec(shape, lambda i, j: (0,) * len(shape))
    vmem = 2 * c * (2 * SSM_INNER + 2 * SSM_BC + LANES) * 4 + 2 * c * SSM_INNER * 2
    vmem += 3 * SSM_PAIRS * LANES * LANES * 4 + (c + 8) * SSM_XBC * 4 + c * SSM_XBC * 4 + c * SSM_INNER * 4
    vmem += 8 << 20
    u, h_out = pl.pallas_call(
        _ssd_kernel,
        grid=(bsz, n_ck),
        in_specs=[row_spec(SSM_INNER, 0), row_spec(SSM_INNER, 1),
                  row_spec(SSM_BC, 2 * xblk), row_spec(SSM_BC, 2 * xblk + 1),
                  row_spec(LANES, 0),
                  const((SSM_CONV, SSM_XBC)), const((1, SSM_XBC)), const((1, LANES)), const((1, LANES)),
                  const((1, SSM_INNER)), const((1, SSM_INNER))],
        out_specs=[pl.BlockSpec((c, SSM_INNER), lambda i, j: (i * n_ck + j, 0)),
                   pl.BlockSpec((1, SSM_PAIRS, LANES, LANES), lambda i, j: (i, 0, 0, 0))],
        out_shape=[jax.ShapeDtypeStruct((bsz * t_len, SSM_INNER), BF16),
                   jax.ShapeDtypeStruct((bsz, SSM_PAIRS, LANES, LANES), F32)],
        scratch_shapes=[pltpu.VMEM((c + 8, SSM_XBC), F32), pltpu.VMEM((c, SSM_XBC), F32),
                        pltpu.VMEM((SSM_PAIRS, LANES, LANES), F32), pltpu.VMEM((c, SSM_INNER), F32)],
        compiler_params=_cparams(("parallel", "arbitrary"), vmem),
        name="ssd_mixer",
    )(zxbc, zxbc, zxbc, zxbc, dt_raw, w_conv.astype(F32), b_conv.reshape(1, SSM_XBC).astype(F32),
      dtb, aneg, dsk, g_out.reshape(1, SSM_INNER).astype(F32))
    return u, h_out.reshape(bsz, SSM_HEADS, SSM_HEAD_DIM, SSM_STATE)


def _ssd_step_kernel(zxbc_ref, dt_ref, h_ref, conv_ref, wconv_ref, bconv_ref, dtb_ref, aneg_ref,
                     dskip_ref, gout_ref, u_ref, hout_ref, convout_ref, y_ref):
    xbc = zxbc_ref[0, :, SSM_INNER:SSM_INNER + SSM_XBC]
    prev = conv_ref[0]
    acc = bconv_ref[...] + wconv_ref[3:4, :] * xbc
    for tap in range(SSM_CONV - 1):
        acc = acc + wconv_ref[tap:tap + 1, :] * prev[tap:tap + 1, :]
    xs = _silu(acc)
    convout_ref[0, 0:2, :] = prev[1:3, :]
    convout_ref[0, 2:3, :] = xbc

    lane1 = lax.broadcasted_iota(jnp.int32, (1, LANES), 1)
    dt = jnp.where(lane1 < SSM_HEADS, _softplus(dt_ref[0] + dtb_ref[...]), 0.0)
    da = jnp.exp(dt * aneg_ref[...])
    lane = lax.broadcasted_iota(jnp.int32, (LANES, LANES), 1)
    row = lax.broadcasted_iota(jnp.int32, (LANES, LANES), 0)
    left1 = lane1 < SSM_HEAD_DIM
    top = row < SSM_HEAD_DIM
    for g in range(SSM_GROUPS):
        bg = xs[:, SSM_INNER + g * SSM_STATE:SSM_INNER + (g + 1) * SSM_STATE]
        cg = xs[:, SSM_INNER + SSM_BC + g * SSM_STATE:SSM_INNER + SSM_BC + (g + 1) * SSM_STATE]
        cb = jnp.sum(cg.astype(BF16).astype(F32) * bg.astype(BF16).astype(F32), axis=-1, keepdims=True)
        cg8 = jnp.broadcast_to(cg, (8, LANES)).astype(BF16)
        for q in range(SSM_PAIRS // SSM_GROUPS):
            pr = g * (SSM_PAIRS // SSM_GROUPS) + q
            h0, h1 = 2 * pr, 2 * pr + 1
            xpair = xs[:, pr * LANES:(pr + 1) * LANES]
            xdt = xpair * jnp.where(left1, dt[:, h0:h0 + 1], dt[:, h1:h1 + 1])
            xdt_col = jnp.broadcast_to(xdt, (LANES, LANES)).T
            hp = h_ref[0, pr]
            keep = jnp.where(top, da[:, h0:h0 + 1], da[:, h1:h1 + 1])
            hout_ref[0, pr] = hp * keep + xdt_col * bg
            y_in = lax.dot_general(cg8, hp.astype(BF16), (((1,), (1,)), ((), ())), preferred_element_type=F32)[0:1]
            y = cb * xdt + y_in * jnp.where(left1, da[:, h0:h0 + 1], da[:, h1:h1 + 1])
            y_ref[:, pr * LANES:(pr + 1) * LANES] = y + dskip_ref[:, pr * LANES:(pr + 1) * LANES] * xpair
    gw = SSM_INNER // SSM_GROUPS
    for g in range(SSM_GROUPS):
        sl = slice(g * gw, (g + 1) * gw)
        u = y_ref[:, sl] * _silu(zxbc_ref[0, :, sl])
        ms = jnp.sum(u * u, axis=-1, keepdims=True) * (1.0 / gw)
        u_ref[0, :, sl] = (u * lax.rsqrt(ms + EPS) * gout_ref[:, sl]).astype(u_ref.dtype)


def ssd_step(zxbc, dt_raw, h_prev, conv_prev, w_conv, b_conv, dt_bias, a_log, d_skip, g_out):
    bsz = zxbc.shape[0]
    pad = LANES - SSM_HEADS
    dtb = jnp.pad(dt_bias.astype(F32), (0, pad)).reshape(1, LANES)
    aneg = jnp.pad(-jnp.exp(a_log.astype(F32)), (0, pad)).reshape(1, LANES)
    dsk = jnp.repeat(d_skip.astype(F32), SSM_HEAD_DIM).reshape(1, SSM_INNER)
    const = lambda shape: pl.BlockSpec(shape, lambda i: (0,) * len(shape))
    st = (1, SSM_PAIRS, LANES, LANES)
    vmem = 4 * SSM_PAIRS * LANES * LANES * 4 + (8 << 20)
    u, h_out, conv_out = pl.pallas_call(
        _ssd_step_kernel,
        grid=(bsz,),
        in_specs=[pl.BlockSpec((1, 1, zxbc.shape[1]), lambda i: (i, 0, 0)),
                  pl.BlockSpec((1, 1, LANES), lambda i: (i, 0, 0)),
                  pl.BlockSpec(st, lambda i: (i, 0, 0, 0)),
                  pl.BlockSpec((1, SSM_CONV - 1, SSM_XBC), lambda i: (i, 0, 0)),
                  const((SSM_CONV, SSM_XBC)), const((1, SSM_XBC)), const((1, LANES)), const((1, LANES)),
                  const((1, SSM_INNER)), const((1, SSM_INNER))],
        out_specs=[pl.BlockSpec((1, 1, SSM_INNER), lambda i: (i, 0, 0)),
                   pl.BlockSpec(st, lambda i: (i, 0, 0, 0)),
                   pl.BlockSpec((1, SSM_CONV - 1, SSM_XBC), lambda i: (i, 0, 0))],
        out_shape=[jax.ShapeDtypeStruct((bsz, 1, SSM_INNER), BF16),
                   jax.ShapeDtypeStruct((bsz, SSM_PAIRS, LANES, LANES), F32),
                   jax.ShapeDtypeStruct((bsz, SSM_CONV - 1, SSM_XBC), F32)],
        scratch_shapes=[pltpu.VMEM((1, SSM_INNER), F32)],
        compiler_params=_cparams(("parallel",), vmem),
        name="ssd_step",
    )(zxbc.reshape(bsz, 1, -1), dt_raw.reshape(bsz, 1, LANES),
      h_prev.astype(F32).reshape(bsz, SSM_PAIRS, LANES, LANES), conv_prev.astype(F32),
      w_conv.astype(F32), b_conv.reshape(1, SSM_XBC).astype(F32), dtb, aneg, dsk,
      g_out.reshape(1, SSM_INNER).astype(F32))
    return u.reshape(bsz, SSM_INNER), h_out.reshape(bsz, SSM_HEADS, SSM_HEAD_DIM, SSM_STATE), conv_out


def _ffn_up_kernel(a_ref, wg_ref, wu_ref, ssq_ref, wc_ref, bc_ref, o_ref, st_ref, gp_ref, g0_ref, g1_ref,
                   u0_ref, u1_ref, *, tiles_per_seq, ni, tm, d_model):
    s = pl.program_id(0)
    i_epi = jnp.maximum(s - 1, 0) % ni

    @pl.when(s == 0)
    def _():
        g1_ref[...] = jnp.zeros_like(g1_ref)
        u1_ref[...] = jnp.zeros_like(u1_ref)

    @pl.when(i_epi % tiles_per_seq == 0)
    def _():
        gp_ref[0:8, :] = jnp.zeros((8, gp_ref.shape[1]), F32)

    def stage(g_new, u_new, g_old, u_old):
        a = a_ref[...]
        g_new[...] = jnp.dot(a, wg_ref[...].astype(BF16), preferred_element_type=F32)
        u_new[...] = jnp.dot(a, wu_ref[...].astype(BF16), preferred_element_type=F32)
        r = _row_scale(ssq_ref, d_model)
        gate = g_old[...] * r
        gp_ref[8:8 + tm, :] = gate
        conv = bc_ref[...] + wc_ref[2:3, :] * gate
        conv = conv + wc_ref[1:2, :] * gp_ref[7:7 + tm, :]
        conv = conv + wc_ref[0:1, :] * gp_ref[6:6 + tm, :]
        o_ref[...] = (_silu(conv) * (u_old[...] * r)).astype(o_ref.dtype)
        st_ref[0] = gp_ref[tm + 6:tm + 8, :]
        gp_ref[0:8, :] = gp_ref[tm:tm + 8, :]

    pl.when(s % 2 == 0)(functools.partial(stage, g0_ref, u0_ref, g1_ref, u1_ref))
    pl.when(s % 2 == 1)(functools.partial(stage, g1_ref, u1_ref, g0_ref, u0_ref))


def ffn_up(xg, row_ssq, bsz, t_len, w_up, w_conv, b_conv, layer, tm=1024, tn=256):
    m, d = xg.shape
    d_ff = w_up.shape[2] // 2
    tm = min(tm, t_len)
    nj, ni = d_ff // tn, m // tm
    assert d_ff % tn == 0 and t_len % tm == 0 and tm % 8 == 0
    tps = t_len // tm
    n_tiles = nj * ni
    cur = lambda s: jnp.minimum(s, n_tiles - 1)
    epi = lambda s: jnp.maximum(s - 1, 0)
    vmem = 2 * tm * d * 2 + 4 * d * tn * 4 + 2 * d * tn * 2 + 2 * tm * tn * 2 + (tm + 8) * tn * 4 + 10 * tm * tn * 4
    kern = functools.partial(_ffn_up_kernel, tiles_per_seq=tps, ni=ni, tm=tm, d_model=d)
    return pl.pallas_call(
        kern,
        grid=(n_tiles + 1,),
        in_specs=[pl.BlockSpec((tm, d), lambda s: (cur(s) % ni, 0)),
                  pl.BlockSpec((None, d, tn), lambda s: (layer, 0, cur(s) // ni)),
                  pl.BlockSpec((None, d, tn), lambda s: (layer, 0, cur(s) // ni + nj)),
                  pl.BlockSpec((tm, LANES), lambda s: (epi(s) % ni, 0)),
                  pl.BlockSpec((None, FFN_CONV, tn), lambda s: (layer, 0, epi(s) // ni)),
                  pl.BlockSpec((None, 1, tn), lambda s: (layer, 0, epi(s) // ni))],
        out_specs=[pl.BlockSpec((tm, tn), lambda s: (epi(s) % ni, epi(s) // ni)),
                   pl.BlockSpec((1, FFN_CONV - 1, tn), lambda s: ((epi(s) % ni) // tps, 0, epi(s) // ni))],
        out_shape=[jax.ShapeDtypeStruct((m, d_ff), BF16),
                   jax.ShapeDtypeStruct((bsz, FFN_CONV - 1, d_ff), F32)],
        scratch_shapes=[pltpu.VMEM((tm + 8, tn), F32)] + [pltpu.VMEM((tm, tn), F32)] * 4,
        compiler_params=_cparams(("arbitrary",), vmem),
        name="ffn_up",
    )(xg, w_up, w_up, row_ssq, w_conv.astype(F32), b_conv.reshape(-1, 1, d_ff).astype(F32))


def _ffn_step_kernel(gu_ref, st_ref, wc_ref, bc_ref, o_ref, stout_ref, *, d_ff):
    gate = gu_ref[:, 0:d_ff]
    up = gu_ref[:, d_ff:2 * d_ff]
    conv = bc_ref[...] + wc_ref[2:3, :] * gate + wc_ref[1:2, :] * st_ref[1] + wc_ref[0:1, :] * st_ref[0]
    o_ref[...] = (_silu(conv) * up).astype(o_ref.dtype)
    stout_ref[0] = st_ref[1]
    stout_ref[1] = gate


def ffn_step(gu, state, w_conv, b_conv):
    bsz, two_dff = gu.shape
    d_ff = two_dff // 2
    a, st = pl.pallas_call(
        functools.partial(_ffn_step_kernel, d_ff=d_ff),
        out_shape=[jax.ShapeDtypeStruct((bsz, d_ff), BF16),
                   jax.ShapeDtypeStruct((FFN_CONV - 1, bsz, d_ff), F32)],
        compiler_params=_cparams((), 16 << 20),
        name="ffn_step",
    )(gu, jnp.swapaxes(state.astype(F32), 0, 1), w_conv.astype(F32), b_conv.reshape(1, d_ff).astype(F32))
    return a, jnp.swapaxes(st, 0, 1)


def _dil_attn_kernel(q0_ref, q1_ref, q2_ref, k0_ref, k1_ref, k2_ref, v0_ref, v1_ref, v2_ref,
                     o_ref, m_ref, l_ref, acc_ref, *, t_len):
    q_refs, k_refs, v_refs = (q0_ref, q1_ref, q2_ref), (k0_ref, k1_ref, k2_ref), (v0_ref, v1_ref, v2_ref)
    scale = HEAD_DIM ** -0.5
    blk = 128
    ii = lax.broadcasted_iota(jnp.int32, (blk, blk), 0)
    jj = lax.broadcasted_iota(jnp.int32, (blk, blk), 1)
    cur_ok = jj <= ii
    prev_ok = jj >= ii
    nt = (((1,), (1,)), ((), ()))
    for g, (w, r) in enumerate(DIL_PATTERNS):
        assert w // r == blk
        n_blk = t_len // (r * blk)
        for cls in range(r):
            for qb in range(n_blk):
                rows = pl.ds(cls + qb * blk * r, blk, stride=r) if r > 1 else pl.ds(qb * blk, blk)
                q = q_refs[g][rows, :].astype(BF16)
                k = k_refs[g][rows, :].astype(BF16)
                v = v_refs[g][rows, :].astype(BF16)
                s = lax.dot_general(q, k, nt, preferred_element_type=F32) * scale
                s = jnp.where(cur_ok, s, -jnp.inf)
                m = jnp.max(s, axis=-1, keepdims=True)
                if qb > 0:
                    prow = (pl.ds(cls + (qb - 1) * blk * r, blk, stride=r) if r > 1
                            else pl.ds((qb - 1) * blk, blk))
                    kp = k_refs[g][prow, :].astype(BF16)
                    vp = v_refs[g][prow, :].astype(BF16)
                    sp = lax.dot_general(q, kp, nt, preferred_element_type=F32) * scale
                    sp = jnp.where(prev_ok, sp, -jnp.inf)
                    m = jnp.maximum(m, jnp.max(sp, axis=-1, keepdims=True))
                p = jnp.exp(s - m)
                den = jnp.sum(p, axis=-1, keepdims=True)
                acc = jnp.dot(p.astype(BF16), v, preferred_element_type=F32)
                if qb > 0:
                    pp = jnp.exp(sp - m)
                    den = den + jnp.sum(pp, axis=-1, keepdims=True)
                    acc = acc + jnp.dot(pp.astype(BF16), vp, preferred_element_type=F32)
                m = jnp.broadcast_to(m, (blk, HEAD_DIM))
                den = jnp.broadcast_to(den, (blk, HEAD_DIM))
                if g == 0:
                    m_ref[rows, :] = m
                    l_ref[rows, :] = den
                    acc_ref[rows, :] = acc
                else:
                    m_old = m_ref[rows, :]
                    m_new = jnp.maximum(m_old, m)
                    a_old = jnp.exp(m_old - m_new)
                    a_new = jnp.exp(m - m_new)
                    m_ref[rows, :] = m_new
                    l_ref[rows, :] = l_ref[rows, :] * a_old + den * a_new
                    acc_ref[rows, :] = acc_ref[rows, :] * a_old + acc * a_new
    o_ref[...] = (acc_ref[...] / l_ref[...]).astype(o_ref.dtype)


def dilated_attention_prompt(qq, kv, bsz, t_len):
    nh = DIL_HEADS
    tspec = lambda colfn: pl.BlockSpec((t_len, HEAD_DIM), lambda b, h: (b, colfn(h)))
    in_specs = ([tspec(lambda h, g=g: g * nh + h) for g in range(3)]
                + [tspec(lambda h, g=g: g * 2 * nh + h) for g in range(3)]
                + [tspec(lambda h, g=g: g * 2 * nh + nh + h) for g in range(3)])
    vmem = 2 * 9 * t_len * HEAD_DIM * 4 + 2 * t_len * HEAD_DIM * 2 + 3 * t_len * HEAD_DIM * 4 + (8 << 20)
    return pl.pallas_call(
        functools.partial(_dil_attn_kernel, t_len=t_len),
        grid=(bsz, nh),
        in_specs=in_specs,
        out_specs=pl.BlockSpec((t_len, HEAD_DIM), lambda b, h: (b, h)),
        out_shape=jax.ShapeDtypeStruct((bsz * t_len, nh * HEAD_DIM), BF16),
        scratch_shapes=[pltpu.VMEM((t_len, HEAD_DIM), F32)] * 3,
        compiler_params=_cparams(("parallel", "parallel"), vmem),
        name="dilated_attention",
    )(qq, qq, qq, kv, kv, kv, kv, kv, kv)


def _dil_step_kernel(q_ref, kvn_ref, c0_ref, c1_ref, c2_ref, o_ref):
    caches = (c0_ref, c1_ref, c2_ref)
    scale = HEAD_DIM ** -0.5
    nh = DIL_HEADS
    s_past, s_new = [], []
    m = None
    for g in range(3):
        q = q_ref[0, g]
        s = jnp.sum(caches[g][0, :, 0:nh, :] * q[None], axis=-1, keepdims=True) * scale
        sn = jnp.sum(kvn_ref[0, g, 0] * q, axis=-1, keepdims=True) * scale
        s_past.append(s)
        s_new.append(sn)
        mg = jnp.maximum(jnp.max(s, axis=0), sn)
        m = mg if m is None else jnp.maximum(m, mg)
    den = jnp.zeros((nh, 1), F32)
    acc = jnp.zeros((nh, HEAD_DIM), F32)
    for g in range(3):
        p = jnp.exp(s_past[g] - m[None])
        pn = jnp.exp(s_new[g] - m)
        den = den + jnp.sum(p, axis=0) + pn
        acc = acc + jnp.sum(p * caches[g][0, :, nh:2 * nh, :], axis=0) + pn * kvn_ref[0, g, 1]
    o_ref[0] = (acc / den).astype(o_ref.dtype)


def dilated_attention_step(q, kv_new, caches):
    bsz = q.shape[0]
    nh = DIL_HEADS
    views = []
    for cache, (w, r) in zip(caches, DIL_PATTERNS):
        assert cache.shape[1] == w, "rolling window cache must hold the full window"
        views.append(cache.reshape(bsz, w // r, r * 2 * nh, HEAD_DIM))
    nkeys = DIL_PATTERNS[0][0] // DIL_PATTERNS[0][1]
    return pl.pallas_call(
        _dil_step_kernel,
        grid=(bsz,),
        in_specs=[pl.BlockSpec((1, 3, nh, HEAD_DIM), lambda b: (b, 0, 0, 0)),
                  pl.BlockSpec((1, 3, 2, nh, HEAD_DIM), lambda b: (b, 0, 0, 0, 0))]
                 + [pl.BlockSpec((1, nkeys, 2 * nh, HEAD_DIM), lambda b: (b, 0, 0, 0)) for _ in DIL_PATTERNS],
        out_specs=pl.BlockSpec((1, nh, HEAD_DIM), lambda b: (b, 0, 0)),
        out_shape=jax.ShapeDtypeStruct((bsz, nh, HEAD_DIM), F32),
        compiler_params=_cparams(("parallel",), 2 * 3 * nkeys * 2 * nh * HEAD_DIM * 4 + (8 << 20)),
        name="dilated_attention_step",
    )(q, kv_new, *views)


def _mem_step_kernel(q_ref, kv_ref, o_ref):
    q = q_ref[0]
    s = jnp.sum(kv_ref[0, :, 0] * q[None], axis=-1, keepdims=True) * (MEM_HEAD_DIM ** -0.5)
    p = jnp.exp(s - jnp.max(s, axis=0)[None])
    o_ref[0] = (jnp.sum(p * kv_ref[0, :, 1], axis=0) / jnp.sum(p, axis=0)).astype(o_ref.dtype)


def mem_attention_step(q, cache, layer):
    bsz = q.shape[0]
    return pl.pallas_call(
        _mem_step_kernel,
        grid=(bsz,),
        in_specs=[pl.BlockSpec((1, MEM_HEADS, MEM_HEAD_DIM), lambda b: (b, 0, 0)),
                  pl.BlockSpec((None, 1, N_MEM, 2, MEM_HEADS, MEM_HEAD_DIM), lambda b: (layer, b, 0, 0, 0, 0))],
        out_specs=pl.BlockSpec((1, MEM_HEADS, MEM_HEAD_DIM), lambda b: (b, 0, 0)),
        out_shape=jax.ShapeDtypeStruct((bsz, MEM_HEADS, MEM_HEAD_DIM), F32),
        compiler_params=_cparams(("parallel",), 4 * N_MEM * 2 * 8 * MEM_HEAD_DIM * 4 + (8 << 20)),
        name="mem_attention_step",
    )(q, cache)


def _trunk(x, bsz, t_len, pos, mem_kv, ssm_prev, conv_prev, ffn_prev, win_past, p):
    m, d = x.shape
    prompt = ssm_prev is None
    rope = rope_tables(pos) if prompt else tuple(jnp.broadcast_to(t, (m, HEAD_DIM)) for t in rope_tables(pos))
    nh = DIL_HEADS
    ffn_states = []

    def conv_ffn(x, xg, ssq, i, emit_gains=()):
        if prompt:
            a, st = ffn_up(xg, ssq, bsz, t_len, p["w_ffn_up"], p["w_ffn_conv"], p["b_ffn_conv"], i)
        else:
            gu = matmul([xg], p["w_ffn_up"], i, n=p["w_ffn_up"].shape[2], row_ssq=(ssq, d), name="ffn_up_raw")
            a, st = ffn_step(gu, ffn_prev[i], p["w_ffn_conv"][i], p["b_ffn_conv"][i])
        ffn_states.append(st)
        return matmul_long_k(a, p["w_ffn_down"], i, x, emit_gains=emit_gains, name="ffn_down")

    def memory_attention(q, q_colblk, layer):
        if prompt:
            return mem_attention(q.reshape(bsz, t_len, -1), q_colblk, mem_kv[layer]).reshape(m, MEM_W)
        q = q[:, q_colblk * MEM_W:(q_colblk + 1) * MEM_W].reshape(m, MEM_HEADS, MEM_HEAD_DIM)
        return mem_attention_step(q, mem_kv, layer).reshape(m, MEM_W)

    wt_in = jnp.swapaxes(p["w_in_a"], 1, 2)
    zx_cols = SSM_INNER + SSM_XBC
    h = rmsnorm_rows(x, p["g_mix"][0])
    zxbc = matmul([h], wt_in, 0, n=zx_cols, w_t=True, name="in_proj_a")
    dt_raw = matmul([h], wt_in, 0, n=LANES, w_col0=zx_cols, tn=LANES, w_t=True, name="in_proj_dt")
    qm = matmul([h], wt_in[:, zx_cols + SSM_HEADS:], 0, n=MEM_W, w_t=True, modes=["norm256"] * 2,
                gain=jnp.tile(p["g_mem_q"][0], MEM_HEADS), name="in_proj_qmem")
    y_mem = memory_attention(qm, 0, 0)
    ssm_args = (p["w_conv_a"][0], p["b_conv_a"][0], p["dt_bias_a"][0], p["a_log_a"][0], p["d_skip_a"][0],
                p["g_ssm_out_a"][0])
    if prompt:
        u, ssm_new = ssd_mixer(zxbc, dt_raw, bsz, t_len, *ssm_args)
        conv_new = zxbc.reshape(bsz, t_len, zx_cols)[:, t_len - (SSM_CONV - 1):, SSM_INNER:]
    else:
        u, ssm_new, conv_new = ssd_step(zxbc, dt_raw, ssm_prev[0], conv_prev[0], *ssm_args)
    x, xg, ssq = matmul([u, y_mem], p["w_out_a"], 0, n=d, res=x, emit_gains=[p["g_ffn"][0]], name="out_proj_a")
    x, xg_kv, xg_mix, ssq = conv_ffn(x, xg, ssq, 0, emit_gains=[p["g_kv"], p["g_mix"][1]])

    gk = jnp.concatenate([jnp.concatenate([jnp.tile(p["g_k_dil"][g], nh), jnp.ones((nh * HEAD_DIM,), F32)])
                          for g in range(3)])
    kv = matmul([xg_kv], p["w_kv"][None], 0, n=6 * nh * HEAD_DIM, row_ssq=(ssq, d),
                modes=(["norm128rope"] * 2 + ["plain"] * 2) * 3, gain=gk, rope=rope, name="kv_proj")

    gq = jnp.concatenate([jnp.tile(p["g_q_dil"][0][g], nh) for g in range(3)]
                         + [jnp.tile(p["g_mem_q"][1], MEM_HEADS)])
    qq = matmul([xg_mix], p["w_in_b"], 0, n=d, row_ssq=(ssq, d), modes=["norm128rope"] * 6 + ["norm256"] * 2,
                gain=gq, rope=rope, name="in_proj_b")
    y_mem = memory_attention(qq, 3, 1)
    if prompt:
        att = dilated_attention_prompt(qq, kv, bsz, t_len)
    else:
        att = dilated_attention_step(qq[:, :3 * nh * HEAD_DIM].reshape(m, 3, nh, HEAD_DIM),
                                     kv.reshape(m, 3, 2, nh, HEAD_DIM), win_past).reshape(m, nh * HEAD_DIM)
    x, xg, ssq = matmul([att, y_mem], p["w_out_b"], 0, n=d, res=x, emit_gains=[p["g_ffn"][1]], name="out_proj_b")
    x = conv_ffn(x, xg, ssq, 1)
    return x, ssm_new, conv_new, jnp.stack(ffn_states, axis=0), kv


def _memory_kv(mem, g_norm, w_kv, layer, g_k):
    bsz, n, d = mem.shape
    hm = rmsnorm_rows(mem.reshape(bsz * n, d), g_norm)
    gain = jnp.concatenate([jnp.tile(g_k, MEM_HEADS), jnp.ones((MEM_W,), F32)])
    kv = matmul([hm], w_kv, layer, n=2 * MEM_W, modes=["norm256"] * 2 + ["plain"] * 2, gain=gain,
                name="mem_kv_proj")
    return kv.reshape(bsz, n, 2 * MEM_W)


def kernel(x_prompt, x_sample, state_ssm, state_ssm_conv, state_ffn_conv, cache_mem_kv, cache_win_kv0, cache_win_kv1, cache_win_kv2, mem_prompt, g_mix, w_in_a, w_conv_a, b_conv_a, dt_bias_a, a_log_a, d_skip_a, g_ssm_out_a, w_out_a, g_kv, w_kv, g_k_dil, w_in_b, g_q_dil, w_out_b, g_mem, w_mem_kv, g_mem_q, g_mem_k, g_ffn, w_ffn_up, w_ffn_conv, b_ffn_conv, w_ffn_down):
    p = dict(g_mix=g_mix, w_in_a=w_in_a, w_conv_a=w_conv_a, b_conv_a=b_conv_a, dt_bias_a=dt_bias_a,
             a_log_a=a_log_a, d_skip_a=d_skip_a, g_ssm_out_a=g_ssm_out_a, w_out_a=w_out_a, g_kv=g_kv, w_kv=w_kv,
             g_k_dil=g_k_dil, w_in_b=w_in_b, g_q_dil=g_q_dil, w_out_b=w_out_b, g_mem_q=g_mem_q, g_ffn=g_ffn,
             w_ffn_up=w_ffn_up, w_ffn_conv=w_ffn_conv, b_ffn_conv=b_ffn_conv, w_ffn_down=w_ffn_down)
    bp, sp, d = x_prompt.shape
    bs, ds, _ = x_sample.shape
    depth = g_mix.shape[0]
    nh = DIL_HEADS

    mem_kv_p = [_memory_kv(mem_prompt, g_mem[i], w_mem_kv, i, g_mem_k[i]) for i in range(depth)]
    y_p, ssm_p, conv_p, ffn_p, kv_p = _trunk(x_prompt.reshape(bp * sp, d), bp, sp, jnp.arange(sp, dtype=jnp.int32),
                                             mem_kv_p, None, None, None, None, p)
    assert ds == 1
    pos_s = PAST_LEN + jnp.arange(ds, dtype=jnp.int32)
    y_s, ssm_s, conv_s, ffn_s, kv_s = _trunk(x_sample.reshape(bs * ds, d), bs, ds, pos_s, cache_mem_kv, state_ssm,
                                             state_ssm_conv, state_ffn_conv,
                                             [cache_win_kv0, cache_win_kv1, cache_win_kv2], p)

    kv_p = kv_p.reshape(bp, sp, 3, 2, nh, HEAD_DIM)
    kv_s = kv_s.reshape(bs, ds, 3, 2, nh, HEAD_DIM)
    win_p = [kv_p[:, sp - min(w, sp):, g] for g, (w, _) in enumerate(DIL_PATTERNS)]
    mem_out = jnp.stack(mem_kv_p, axis=0).reshape(depth, bp, N_MEM, 2, MEM_HEADS, MEM_HEAD_DIM)
    return (y_p.reshape(bp, sp, d), y_s.reshape(bs, ds, d), ssm_p[None], ssm_s[None], conv_p[None], conv_s[None],
            ffn_p, ffn_s, mem_out, win_p[0], win_p[1], win_p[2], kv_s[:, :, 0], kv_s[:, :, 1], kv_s[:, :, 2])
```

```python
import functools
import math

import jax
import jax.numpy as jnp
from jax import lax
from jax.experimental import pallas as pl
from jax.experimental.pallas import tpu as pltpu

F32 = jnp.float32
BF16 = jnp.bfloat16
EPS = 1e-6
LANES = 128
V7X_VMEM_BYTES = 64 * 2**20

SSM_HEAD_DIM = 64
SSM_HEADS = 48
SSM_GROUPS = 8
SSM_STATE = 128
SSM_INNER = SSM_HEADS * SSM_HEAD_DIM
SSM_BC = SSM_GROUPS * SSM_STATE
SSM_XBC = SSM_INNER + 2 * SSM_BC
SSM_CONV = 4
SSM_CHUNK = 128
SSM_PAIRS = SSM_HEADS // 2
DIL_PATTERNS = ((128, 1), (512, 4), (2048, 16))
DIL_HEADS = 8
HEAD_DIM = 128
ROT_DIM = HEAD_DIM // 4
ROPE_THETA = 500000.0
N_MEM = 256
MEM_HEADS = 4
MEM_HEAD_DIM = 256
MEM_W = MEM_HEADS * MEM_HEAD_DIM
FFN_CONV = 3
PAST_LEN = 16384


def _cparams(sem, vmem_bytes):
    limit = min(int(vmem_bytes * 1.25) + (4 << 20), V7X_VMEM_BYTES - (6 << 20))
    return pltpu.CompilerParams(dimension_semantics=sem or None, vmem_limit_bytes=limit)


def _silu(x):
    return x * jax.nn.sigmoid(x)


def _softplus(x):
    return jnp.maximum(x, 0.0) + jnp.log1p(jnp.exp(-jnp.abs(x)))


def _rmsnorm_kernel(x_ref, g_ref, o_ref):
    x = x_ref[...]
    ms = jnp.mean(x * x, axis=-1, keepdims=True)
    o_ref[...] = (x * lax.rsqrt(ms + EPS) * g_ref[...]).astype(o_ref.dtype)


def rmsnorm_rows(x, g, tm=256):
    m, d = x.shape
    tm = min(tm, m)
    return pl.pallas_call(
        _rmsnorm_kernel,
        grid=(m // tm,),
        in_specs=[pl.BlockSpec((tm, d), lambda i: (i, 0)),
                  pl.BlockSpec((1, d), lambda i: (0, 0))],
        out_specs=pl.BlockSpec((tm, d), lambda i: (i, 0)),
        out_shape=jax.ShapeDtypeStruct((m, d), BF16),
        compiler_params=_cparams(("parallel",), 2 * tm * d * 6),
        name="rmsnorm_rows",
    )(x, g.reshape(1, d).astype(F32))


def rope_tables(pos):
    half = ROT_DIM // 2
    inv_freq = jnp.exp(-(2.0 * jnp.arange(half, dtype=F32) / ROT_DIM) * math.log(ROPE_THETA))
    ang = pos.astype(F32)[:, None] * inv_freq[None, :]
    cos, sin = jnp.cos(ang), jnp.sin(ang)
    n = pos.shape[0]
    ones = jnp.ones((n, HEAD_DIM - ROT_DIM), F32)
    zeros = jnp.zeros((n, HEAD_DIM - ROT_DIM), F32)
    zh = jnp.zeros((n, half), F32)
    c = jnp.concatenate([cos, cos, ones], axis=1)
    s_lo = jnp.concatenate([-sin, zh, zeros], axis=1)
    s_hi = jnp.concatenate([zh, sin, zeros], axis=1)
    return c, s_lo, s_hi


def _head_norm(blk, gain, hd):
    ms = jnp.sum(blk * blk, axis=-1, keepdims=True) * (1.0 / hd)
    return blk * lax.rsqrt(ms + EPS) * gain


def _row_scale(ssq_ref, d_model):
    return lax.rsqrt(jnp.sum(ssq_ref[...], axis=-1, keepdims=True) * (1.0 / d_model) + EPS)


def _lane_partial_ssq(y):
    y2 = y * y
    part = y2[:, 0:LANES]
    for c0 in range(LANES, y.shape[1], LANES):
        part = part + y2[:, c0:c0 + LANES]
    return part


def _mm_kernel(*refs, n_a, modes, nj, n_tiles, tn, w_t, has_gain, has_rope, has_res, d_scale, n_emit,
               pipelined):
    it = iter(refs)
    a_refs = [next(it) for _ in range(n_a)]
    w_refs = [next(it) for _ in range(n_a)]
    ssq_in_ref = next(it) if d_scale else None
    gain_ref = next(it) if has_gain else None
    rope_refs = (next(it), next(it), next(it)) if has_rope else None
    res_ref = next(it) if has_res else None
    emit_gain_refs = [next(it) for _ in range(n_emit)]
    o_ref = next(it)
    emit_refs = [next(it) for _ in range(n_emit)]
    ssq_out_ref = next(it) if n_emit else None
    p_bufs = (next(it), next(it)) if pipelined else None
    s = pl.program_id(0)
    j_epi = (jnp.maximum(s - 1, 0) if pipelined else s) % nj

    def product():
        p = None
        for a_ref, w_ref in zip(a_refs, w_refs):
            wb = w_ref[...].astype(BF16)
            dims = (((1,), (1,)), ((), ())) if w_t else (((1,), (0,)), ((), ()))
            d = lax.dot_general(a_ref[...].astype(BF16), wb, dims, preferred_element_type=F32)
            p = d if p is None else p + d
        return p

    def epilogue(mode, p):
        if d_scale:
            p = p * _row_scale(ssq_in_ref, d_scale)
        if mode == "plain":
            y = p + res_ref[...] if has_res else p
            o_ref[...] = y.astype(o_ref.dtype)
            if n_emit:
                for g_ref, e_ref in zip(emit_gain_refs, emit_refs):
                    e_ref[...] = (y * g_ref[...]).astype(e_ref.dtype)
                part = _lane_partial_ssq(y)

                @pl.when(j_epi == 0)
                def _():
                    ssq_out_ref[...] = part

                @pl.when(j_epi > 0)
                def _():
                    ssq_out_ref[...] += part
            return
        hd = 256 if mode == "norm256" else 128
        for h0 in range(0, tn, hd):
            y = _head_norm(p[:, h0:h0 + hd], gain_ref[:, h0:h0 + hd], hd)
            if mode == "norm128rope":
                c_ref, slo_ref, shi_ref = rope_refs
                half = ROT_DIM // 2
                y = (y * c_ref[...]
                     + pltpu.roll(y, hd - half, 1) * slo_ref[...]
                     + pltpu.roll(y, half, 1) * shi_ref[...])
            o_ref[:, h0:h0 + hd] = y.astype(o_ref.dtype)

    def stage(mode, parity):
        if pipelined:
            new, old = (p_bufs[0], p_bufs[1]) if parity == 0 else (p_bufs[1], p_bufs[0])
            new[...] = product()
            epilogue(mode, old[...])
        else:
            epilogue(mode, product())

    if pipelined:
        @pl.when(s == 0)
        def _():
            p_bufs[1][...] = jnp.zeros_like(p_bufs[1])

    distinct = sorted(set(modes))
    for mode in distinct:
        cond = None
        if len(distinct) > 1:
            for jj, mj in enumerate(modes):
                if mj == mode:
                    c = j_epi == jj
                    cond = c if cond is None else jnp.logical_or(cond, c)
        for parity in ((0, 1) if pipelined else (None,)):
            c = cond
            if parity is not None:
                cp = (s % 2) == parity
                c = cp if c is None else jnp.logical_and(c, cp)
            if c is None:
                stage(mode, parity)
            else:
                pl.when(c)(functools.partial(stage, mode, parity))


def matmul(a_list, w, layer, *, n, tm=1024, tn=512, w_col0=0, w_t=False, out_dtype=F32, modes=None,
           gain=None, rope=None, res=None, row_ssq=None, emit_gains=(), name="matmul"):
    m = a_list[0].shape[0]
    tm = min(tm, m)
    assert m % tm == 0 and n % tn == 0 and w_col0 % tn == 0
    nj = n // tn
    n_tiles = (m // tm) * nj
    modes = tuple(modes) if modes is not None else ("plain",) * nj
    assert len(modes) == nj
    pipelined = n_tiles >= 16 and "norm128rope" in modes
    cb = w_col0 // tn
    cur = (lambda s: jnp.minimum(s, n_tiles - 1)) if pipelined else (lambda s: s)
    epi = (lambda s: jnp.maximum(s - 1, 0)) if pipelined else (lambda s: s)
    in_specs, w_specs, vmem, row0 = [], [], 0, 0
    for a in a_list:
        ki = a.shape[1]
        assert row0 % ki == 0
        rb = row0 // ki
        in_specs.append(pl.BlockSpec((tm, ki), lambda s: (cur(s) // nj, 0), pipeline_mode=pl.Buffered(1)))
        if w_t:
            w_specs.append(pl.BlockSpec((None, tn, ki), lambda s, rb=rb: (layer, cur(s) % nj + cb, rb)))
        else:
            w_specs.append(pl.BlockSpec((None, ki, tn), lambda s, rb=rb: (layer, rb, cur(s) % nj + cb)))
        vmem += tm * ki * a.dtype.itemsize + tm * ki * 2 + 2 * ki * tn * 4 + ki * tn * 2
        row0 += ki
    assert row0 == w.shape[2 if w_t else 1]
    in_specs += w_specs
    args = list(a_list) + [w] * len(a_list)
    vmem += 2 * tm * tn * jnp.dtype(out_dtype).itemsize + 4 * tm * tn * 4
    tile_spec = pl.BlockSpec((tm, tn), lambda s: (epi(s) // nj, epi(s) % nj))
    col_spec = pl.BlockSpec((1, tn), lambda s: (0, epi(s) % nj))
    ssq_spec = pl.BlockSpec((tm, LANES), lambda s: (epi(s) // nj, 0))
    if row_ssq is not None:
        in_specs.append(ssq_spec)
        args.append(row_ssq[0])
    if gain is not None:
        in_specs.append(col_spec)
        args.append(gain.reshape(1, n).astype(F32))
    if rope is not None:
        period = rope[0].shape[0]
        assert period % tm == 0
        nper = period // tm
        for t in rope:
            in_specs.append(pl.BlockSpec((tm, HEAD_DIM), lambda s: ((epi(s) // nj) % nper, 0)))
            args.append(t)
    if res is not None:
        in_specs.append(tile_spec)
        args.append(res)
        vmem += 2 * tm * tn * 4
    out_specs = [tile_spec]
    out_shape = [jax.ShapeDtypeStruct((m, n), out_dtype)]
    if emit_gains:
        assert nj * tn == n and all(g.shape == (n,) for g in emit_gains)
        for g in emit_gains:
            in_specs.append(col_spec)
            args.append(g.reshape(1, n).astype(F32))
            out_specs.append(tile_spec)
            out_shape.append(jax.ShapeDtypeStruct((m, n), BF16))
            vmem += 2 * tm * tn * 2
        out_specs.append(ssq_spec)
        out_shape.append(jax.ShapeDtypeStruct((m, LANES), F32))
    kern = functools.partial(_mm_kernel, n_a=len(a_list), modes=modes, nj=nj, n_tiles=n_tiles, tn=tn, w_t=w_t,
                             has_gain=gain is not None, has_rope=rope is not None, has_res=res is not None,
                             d_scale=row_ssq[1] if row_ssq is not None else 0, n_emit=len(emit_gains),
                             pipelined=pipelined)
    outs = pl.pallas_call(
        kern,
        grid=(n_tiles + 1 if pipelined else n_tiles,),
        in_specs=in_specs,
        out_specs=out_specs,
        out_shape=out_shape,
        scratch_shapes=[pltpu.VMEM((tm, tn), F32)] * 2 if pipelined else [],
        compiler_params=_cparams(("arbitrary",), vmem),
        name=name,
    )(*args)
    return outs if emit_gains else outs[0]


def _mm_panel_kernel(*refs, nk, n_emit):
    it = iter(refs)
    a_ref, w_ref, res_ref, a2_ref, res2_ref = (next(it) for _ in range(5))
    emit_gain_refs = [next(it) for _ in range(n_emit)]
    o_ref = next(it)
    emit_refs = [next(it) for _ in range(n_emit)]
    ssq_out_ref = next(it) if n_emit else None
    o2_ref = next(it)
    emit2_refs = [next(it) for _ in range(n_emit)]
    ssq2_out_ref = next(it) if n_emit else None
    acc_ref, acc2_ref = next(it), next(it)
    i, k, j = pl.program_id(0), pl.program_id(1), pl.program_id(2)
    wb = w_ref[...].astype(BF16)

    def k_pass(p, acc, finish):
        @pl.when(k == 0)
        def _():
            acc[j] = p

        if nk > 2:
            @pl.when(jnp.logical_and(k > 0, k < nk - 1))
            def _():
                acc[j] += p

        @pl.when(k == nk - 1)
        def _():
            finish(acc[j] + p)

    def finisher(res, out, emits, ssq_out):
        def finish(total):
            y = total + res[...]
            out[...] = y.astype(out.dtype)
            if n_emit:
                for g_ref, e_ref in zip(emit_gain_refs, emits):
                    e_ref[...] = (y * g_ref[...]).astype(e_ref.dtype)
                part = _lane_partial_ssq(y)

                @pl.when(j == 0)
                def _():
                    ssq_out[...] = part

                @pl.when(j > 0)
                def _():
                    ssq_out[...] += part
        return finish

    k_pass(jnp.dot(a_ref[...].astype(BF16), wb, preferred_element_type=F32), acc_ref,
           finisher(res_ref, o_ref, emit_refs, ssq_out_ref))

    @pl.when(i == 0)
    def _():
        k_pass(jnp.dot(a2_ref[...].astype(BF16), wb, preferred_element_type=F32), acc2_ref,
               finisher(res2_ref, o2_ref, emit2_refs, ssq2_out_ref))


def matmul_long_k(a, w, layer, res, a2, res2, *, tm=1024, tn=256, nk=2, emit_gains=(), name="matmul_long_k"):
    m, kdim = a.shape
    b2 = a2.shape[0]
    n = w.shape[2]
    tm = min(tm, m)
    assert kdim % nk == 0 and (kdim // nk) % LANES == 0 and m % tm == 0 and n % tn == 0 and nk >= 2
    tk = kdim // nk
    nj = n // tn
    last = lambda k, j: jnp.where(k == nk - 1, j, 0)
    park = lambda i, k, j: jnp.where(i == 0, last(k, j), nj - 1)
    tile_spec = pl.BlockSpec((tm, tn), lambda i, k, j: (i, last(k, j)))
    tile2_spec = pl.BlockSpec((b2, tn), lambda i, k, j: (0, park(i, k, j)))
    col_spec = pl.BlockSpec((1, tn), lambda i, k, j: (0, last(k, j)))
    vmem = tm * tk * 2 + 2 * tk * tn * 4 + tk * tn * 2 + nj * tm * tn * 4 + 5 * tm * tn * 4
    in_specs = [pl.BlockSpec((tm, tk), lambda i, k, j: (i, k), pipeline_mode=pl.Buffered(1)),
                pl.BlockSpec((None, tk, tn), lambda i, k, j: (layer, k, j)),
                tile_spec,
                pl.BlockSpec((b2, tk), lambda i, k, j: (0, k)),
                tile2_spec]
    args = [a, w, res, a2, res2]
    out_specs, out2_specs = [tile_spec], [tile2_spec]
    out_shape, out2_shape = [jax.ShapeDtypeStruct((m, n), F32)], [jax.ShapeDtypeStruct((b2, n), F32)]
    for g in emit_gains:
        in_specs.append(col_spec)
        args.append(g.reshape(1, n).astype(F32))
        out_specs.append(tile_spec)
        out_shape.append(jax.ShapeDtypeStruct((m, n), BF16))
        out2_specs.append(tile2_spec)
        out2_shape.append(jax.ShapeDtypeStruct((b2, n), F32))
        vmem += 2 * tm * tn * 2
    if emit_gains:
        out_specs.append(pl.BlockSpec((tm, LANES), lambda i, k, j: (i, 0)))
        out_shape.append(jax.ShapeDtypeStruct((m, LANES), F32))
        out2_specs.append(pl.BlockSpec((b2, LANES), lambda i, k, j: (0, 0)))
        out2_shape.append(jax.ShapeDtypeStruct((b2, LANES), F32))
    outs = pl.pallas_call(
        functools.partial(_mm_panel_kernel, nk=nk, n_emit=len(emit_gains)),
        grid=(m // tm, nk, nj),
        in_specs=in_specs,
        out_specs=out_specs + out2_specs,
        out_shape=out_shape + out2_shape,
        scratch_shapes=[pltpu.VMEM((nj, tm, tn), F32), pltpu.VMEM((nj, b2, tn), F32)],
        compiler_params=_cparams(("arbitrary", "arbitrary", "arbitrary"), vmem),
        name=name,
    )(*args)
    n1 = len(out_specs)
    return tuple(outs[:n1]), tuple(outs[n1:])


def _mem_attn_kernel(q_ref, kv_ref, o_ref):
    scale = MEM_HEAD_DIM ** -0.5
    for h in range(MEM_HEADS):
        lo = h * MEM_HEAD_DIM
        q = q_ref[0, :, lo:lo + MEM_HEAD_DIM].astype(BF16)
        tq = q.shape[0]
        if tq < 8:
            q = jnp.broadcast_to(q, (8, MEM_HEAD_DIM))
        k = kv_ref[0, :, lo:lo + MEM_HEAD_DIM].astype(BF16)
        v = kv_ref[0, :, MEM_W + lo:MEM_W + lo + MEM_HEAD_DIM].astype(BF16)
        s = lax.dot_general(q, k, (((1,), (1,)), ((), ())), preferred_element_type=F32) * scale
        p = jnp.exp(s - jnp.max(s, axis=-1, keepdims=True))
        den = jnp.sum(p, axis=-1, keepdims=True)
        o = jnp.dot(p.astype(BF16), v, preferred_element_type=F32) / den
        o_ref[0, :, lo:lo + MEM_HEAD_DIM] = o[0:tq].astype(o_ref.dtype)


def mem_attention(q, q_colblk, mem_kv, tq=512):
    b, t, _ = q.shape
    tq = min(tq, t)
    return pl.pallas_call(
        _mem_attn_kernel,
        grid=(b, t // tq),
        in_specs=[pl.BlockSpec((1, tq, MEM_W), lambda i, j: (i, j, q_colblk)),
                  pl.BlockSpec((1, N_MEM, 2 * MEM_W), lambda i, j: (i, 0, 0))],
        out_specs=pl.BlockSpec((1, tq, MEM_W), lambda i, j: (i, j, 0)),
        out_shape=jax.ShapeDtypeStruct((b, t, MEM_W), BF16),
        compiler_params=_cparams(("parallel", "parallel"), 2 * (tq * MEM_W * 6 + N_MEM * 2 * MEM_W * 4) + 8 * tq * N_MEM * 4),
        name="mem_attention",
    )(q, mem_kv)


def _split3_dot(lhs_bf16, x):
    hi = x.astype(BF16)
    r1 = x - hi.astype(F32)
    mid = r1.astype(BF16)
    lo = (r1 - mid.astype(F32)).astype(BF16)
    out = jnp.dot(lhs_bf16, hi, preferred_element_type=F32)
    out += jnp.dot(lhs_bf16, mid, preferred_element_type=F32)
    out += jnp.dot(lhs_bf16, lo, preferred_element_type=F32)
    return out


def _ssd_kernel(z_ref, x_ref, b_ref, c_ref, dt_ref, wconv_ref, bconv_ref, dtb_ref, aneg_ref,
                dskip_ref, gout_ref, u_ref, hout_ref, xp_ref, xs_ref, h_ref, y_ref):
    ck = pl.program_id(1)
    n_ck = pl.num_programs(1)
    c = SSM_CHUNK

    @pl.when(ck == 0)
    def _():
        xp_ref[0:8, :] = jnp.zeros((8, SSM_XBC), F32)
        h_ref[...] = jnp.zeros_like(h_ref)

    xp_ref[8:8 + c, 0:SSM_INNER] = x_ref[...]
    xp_ref[8:8 + c, SSM_INNER:SSM_INNER + SSM_BC] = b_ref[...]
    xp_ref[8:8 + c, SSM_INNER + SSM_BC:SSM_XBC] = c_ref[...]

    slab = 512
    for c0 in range(0, SSM_XBC, slab):
        acc = bconv_ref[:, c0:c0 + slab] + wconv_ref[3:4, c0:c0 + slab] * xp_ref[8:8 + c, c0:c0 + slab]
        for tap in range(SSM_CONV - 1):
            off = 8 - (SSM_CONV - 1 - tap)
            acc = acc + wconv_ref[tap:tap + 1, c0:c0 + slab] * xp_ref[off:off + c, c0:c0 + slab]
        xs_ref[:, c0:c0 + slab] = _silu(acc)
    xp_ref[0:8, :] = xp_ref[c:c + 8, :]

    lane = lax.broadcasted_iota(jnp.int32, (c, LANES), 1)
    row = lax.broadcasted_iota(jnp.int32, (c, LANES), 0)
    dt = jnp.where(lane < SSM_HEADS, _softplus(dt_ref[...] + dtb_ref[...]), 0.0)
    la = dt * aneg_ref[...]
    tri = row >= lane
    cum = _split3_dot(jnp.where(tri, 1.0, 0.0).astype(BF16), la)
    cum_t = cum.T
    ecum = jnp.exp(cum)
    wend = jnp.exp(cum[c - 1:c, :] - cum)
    elast_t = jnp.exp(cum_t[:, c - 1:c])
    left = lane < SSM_HEAD_DIM
    top = row < SSM_HEAD_DIM

    def pair(col0, col1):
        return jnp.where(left, col0, col1)

    for g in range(SSM_GROUPS):
        bg = xs_ref[:, SSM_INNER + g * SSM_STATE:SSM_INNER + (g + 1) * SSM_STATE].astype(BF16)
        cg = xs_ref[:, SSM_INNER + SSM_BC + g * SSM_STATE:SSM_INNER + SSM_BC + (g + 1) * SSM_STATE].astype(BF16)
        cb = lax.dot_general(cg, bg, (((1,), (1,)), ((), ())), preferred_element_type=F32)
        for q in range(SSM_PAIRS // SSM_GROUPS):
            pr = g * (SSM_PAIRS // SSM_GROUPS) + q
            h0, h1 = 2 * pr, 2 * pr + 1
            xpair = xs_ref[:, pr * LANES:(pr + 1) * LANES]
            xdt = xpair * pair(dt[:, h0:h0 + 1], dt[:, h1:h1 + 1])
            xdt_b = xdt.astype(BF16)
            ys = []
            for hh in (h0, h1):
                seg = cum[:, hh:hh + 1] - cum_t[hh:hh + 1, :]
                decay = jnp.exp(jnp.where(tri, seg, -jnp.inf))
                ys.append(jnp.dot((cb * decay).astype(BF16), xdt_b, preferred_element_type=F32))
            y = pair(ys[0], ys[1])
            hp = h_ref[pr]
            y_in = lax.dot_general(cg, hp.astype(BF16), (((1,), (1,)), ((), ())), preferred_element_type=F32)
            y = y + y_in * pair(ecum[:, h0:h0 + 1], ecum[:, h1:h1 + 1])
            xw = xdt * pair(wend[:, h0:h0 + 1], wend[:, h1:h1 + 1])
            upd = jnp.dot(xw.T.astype(BF16), bg, preferred_element_type=F32)
            keep = jnp.where(top, elast_t[h0:h0 + 1, :], elast_t[h1:h1 + 1, :])
            h_ref[pr] = hp * keep + upd
            y_ref[:, pr * LANES:(pr + 1) * LANES] = y + dskip_ref[:, pr * LANES:(pr + 1) * LANES] * xpair

    gw = SSM_INNER // SSM_GROUPS
    for g in range(SSM_GROUPS):
        sl = slice(g * gw, (g + 1) * gw)
        u = y_ref[:, sl] * _silu(z_ref[:, sl])
        ms = jnp.sum(u * u, axis=-1, keepdims=True) * (1.0 / gw)
        u_ref[:, sl] = (u * lax.rsqrt(ms + EPS) * gout_ref[:, sl]).astype(u_ref.dtype)

    @pl.when(ck == n_ck - 1)
    def _():
        hout_ref[0] = h_ref[...]


def ssd_mixer(zxbc, dt_raw, bsz, t_len, w_conv, b_conv, dt_bias, a_log, d_skip, g_out):
    c = SSM_CHUNK
    n_ck = t_len // c
    xblk = SSM_INNER // SSM_BC
    pad = LANES - SSM_HEADS
    dtb = jnp.pad(dt_bias.astype(F32), (0, pad)).reshape(1, LANES)
    aneg = jnp.pad(-jnp.exp(a_log.astype(F32)), (0, pad)).reshape(1, LANES)
    dsk = jnp.repeat(d_skip.astype(F32), SSM_HEAD_DIM).reshape(1, SSM_INNER)
    row_spec = lambda width, blk: pl.BlockSpec((c, width), lambda i, j: (i * n_ck + j, blk))
    const = lambda shape: pl.BlockSpec(shape, lambda i, j: (0,) * len(shape))
    vmem = 2 * c * (2 * SSM_INNER + 2 * SSM_BC + LANES) * 4 + 2 * c * SSM_INNER * 2
    vmem += 3 * SSM_PAIRS * LANES * LANES * 4 + (c + 8) * SSM_XBC * 4 + c * SSM_XBC * 4 + c * SSM_INNER * 4
    vmem += 8 << 20
    u, h_out = pl.pallas_call(
        _ssd_kernel,
        grid=(bsz, n_ck),
        in_specs=[row_spec(SSM_INNER, 0), row_spec(SSM_INNER, 1),
                  row_spec(SSM_BC, 2 * xblk), row_spec(SSM_BC, 2 * xblk + 1),
                  row_spec(LANES, 0),
                  const((SSM_CONV, SSM_XBC)), const((1, SSM_XBC)), const((1, LANES)), const((1, LANES)),
                  const((1, SSM_INNER)), const((1, SSM_INNER))],
        out_specs=[pl.BlockSpec((c, SSM_INNER), lambda i, j: (i * n_ck + j, 0)),
                   pl.BlockSpec((1, SSM_PAIRS, LANES, LANES), lambda i, j: (i, 0, 0, 0))],
        out_shape=[jax.ShapeDtypeStruct((bsz * t_len, SSM_INNER), BF16),
                   jax.ShapeDtypeStruct((bsz, SSM_PAIRS, LANES, LANES), F32)],
        scratch_shapes=[pltpu.VMEM((c + 8, SSM_XBC), F32), pltpu.VMEM((c, SSM_XBC), F32),
                        pltpu.VMEM((SSM_PAIRS, LANES, LANES), F32), pltpu.VMEM((c, SSM_INNER), F32)],
        compiler_params=_cparams(("parallel", "arbitrary"), vmem),
        name="ssd_mixer",
    )(zxbc, zxbc, zxbc, zxbc, dt_raw, w_conv.astype(F32), b_conv.reshape(1, SSM_XBC).astype(F32),
      dtb, aneg, dsk, g_out.reshape(1, SSM_INNER).astype(F32))
    return u, h_out.reshape(bsz, SSM_HEADS, SSM_HEAD_DIM, SSM_STATE)


def _ssd_step_kernel(zxbc_ref, dt_ref, h_ref, conv_ref, wconv_ref, bconv_ref, dtb_ref, aneg_ref,
                     dskip_ref, gout_ref, u_ref, hout_ref, convout_ref, y_ref):
    xbc = zxbc_ref[0, :, SSM_INNER:SSM_INNER + SSM_XBC]
    prev = conv_ref[0]
    acc = bconv_ref[...] + wconv_ref[3:4, :] * xbc
    for tap in range(SSM_CONV - 1):
        acc = acc + wconv_ref[tap:tap + 1, :] * prev[tap:tap + 1, :]
    xs = _silu(acc)
    convout_ref[0, 0:2, :] = prev[1:3, :]
    convout_ref[0, 2:3, :] = xbc

    lane1 = lax.broadcasted_iota(jnp.int32, (1, LANES), 1)
    dt = jnp.where(lane1 < SSM_HEADS, _softplus(dt_ref[0] + dtb_ref[...]), 0.0)
    da = jnp.exp(dt * aneg_ref[...])
    lane = lax.broadcasted_iota(jnp.int32, (LANES, LANES), 1)
    row = lax.broadcasted_iota(jnp.int32, (LANES, LANES), 0)
    left1 = lane1 < SSM_HEAD_DIM
    top = row < SSM_HEAD_DIM
    for g in range(SSM_GROUPS):
        bg = xs[:, SSM_INNER + g * SSM_STATE:SSM_INNER + (g + 1) * SSM_STATE]
        cg = xs[:, SSM_INNER + SSM_BC + g * SSM_STATE:SSM_INNER + SSM_BC + (g + 1) * SSM_STATE]
        cb = jnp.sum(cg.astype(BF16).astype(F32) * bg.astype(BF16).astype(F32), axis=-1, keepdims=True)
        cg8 = jnp.broadcast_to(cg, (8, LANES)).astype(BF16)
        for q in range(SSM_PAIRS // SSM_GROUPS):
            pr = g * (SSM_PAIRS // SSM_GROUPS) + q
            h0, h1 = 2 * pr, 2 * pr + 1
            xpair = xs[:, pr * LANES:(pr + 1) * LANES]
            xdt = xpair * jnp.where(left1, dt[:, h0:h0 + 1], dt[:, h1:h1 + 1])
            xdt_col = jnp.broadcast_to(xdt, (LANES, LANES)).T
            hp = h_ref[0, pr]
            keep = jnp.where(top, da[:, h0:h0 + 1], da[:, h1:h1 + 1])
            hout_ref[0, pr] = hp * keep + xdt_col * bg
            y_in = lax.dot_general(cg8, hp.astype(BF16), (((1,), (1,)), ((), ())), preferred_element_type=F32)[0:1]
            y = cb * xdt + y_in * jnp.where(left1, da[:, h0:h0 + 1], da[:, h1:h1 + 1])
            y_ref[:, pr * LANES:(pr + 1) * LANES] = y + dskip_ref[:, pr * LANES:(pr + 1) * LANES] * xpair
    gw = SSM_INNER // SSM_GROUPS
    for g in range(SSM_GROUPS):
        sl = slice(g * gw, (g + 1) * gw)
        u = y_ref[:, sl] * _silu(zxbc_ref[0, :, sl])
        ms = jnp.sum(u * u, axis=-1, keepdims=True) * (1.0 / gw)
        u_ref[0, :, sl] = (u * lax.rsqrt(ms + EPS) * gout_ref[:, sl]).astype(u_ref.dtype)


def ssd_step(zxbc, dt_raw, h_prev, conv_prev, w_conv, b_conv, dt_bias, a_log, d_skip, g_out):
    bsz = zxbc.shape[0]
    pad = LANES - SSM_HEADS
    dtb = jnp.pad(dt_bias.astype(F32), (0, pad)).reshape(1, LANES)
    aneg = jnp.pad(-jnp.exp(a_log.astype(F32)), (0, pad)).reshape(1, LANES)
    dsk = jnp.repeat(d_skip.astype(F32), SSM_HEAD_DIM).reshape(1, SSM_INNER)
    const = lambda shape: pl.BlockSpec(shape, lambda i: (0,) * len(shape))
    st = (1, SSM_PAIRS, LANES, LANES)
    vmem = 4 * SSM_PAIRS * LANES * LANES * 4 + (8 << 20)
    u, h_out, conv_out = pl.pallas_call(
        _ssd_step_kernel,
        grid=(bsz,),
        in_specs=[pl.BlockSpec((1, 1, zxbc.shape[1]), lambda i: (i, 0, 0)),
                  pl.BlockSpec((1, 1, LANES), lambda i: (i, 0, 0)),
                  pl.BlockSpec(st, lambda i: (i, 0, 0, 0)),
                  pl.BlockSpec((1, SSM_CONV - 1, SSM_XBC), lambda i: (i, 0, 0)),
                  const((SSM_CONV, SSM_XBC)), const((1, SSM_XBC)), const((1, LANES)), const((1, LANES)),
                  const((1, SSM_INNER)), const((1, SSM_INNER))],
        out_specs=[pl.BlockSpec((1, 1, SSM_INNER), lambda i: (i, 0, 0)),
                   pl.BlockSpec(st, lambda i: (i, 0, 0, 0)),
                   pl.BlockSpec((1, SSM_CONV - 1, SSM_XBC), lambda i: (i, 0, 0))],
        out_shape=[jax.ShapeDtypeStruct((bsz, 1, SSM_INNER), BF16),
                   jax.ShapeDtypeStruct((bsz, SSM_PAIRS, LANES, LANES), F32),
                   jax.ShapeDtypeStruct((bsz, SSM_CONV - 1, SSM_XBC), F32)],
        scratch_shapes=[pltpu.VMEM((1, SSM_INNER), F32)],
        compiler_params=_cparams(("parallel",), vmem),
        name="ssd_step",
    )(zxbc.reshape(bsz, 1, -1), dt_raw.reshape(bsz, 1, LANES),
      h_prev.astype(F32).reshape(bsz, SSM_PAIRS, LANES, LANES), conv_prev.astype(F32),
      w_conv.astype(F32), b_conv.reshape(1, SSM_XBC).astype(F32), dtb, aneg, dsk,
      g_out.reshape(1, SSM_INNER).astype(F32))
    return u.reshape(bsz, SSM_INNER), h_out.reshape(bsz, SSM_HEADS, SSM_HEAD_DIM, SSM_STATE), conv_out


def _ffn_up_kernel(a_ref, wg_ref, wu_ref, ssq_ref, wc_ref, bc_ref, a2_ref, ssq2_ref, st2_ref,
                   o_ref, st_ref, o2_ref, st2o_ref, gp_ref, *, tiles_per_seq, tm, d_model):
    i = pl.program_id(1)

    @pl.when(i % tiles_per_seq == 0)
    def _():
        gp_ref[0:8, :] = jnp.zeros((8, gp_ref.shape[1]), F32)

    @pl.when(i == 0)
    def _():
        a2 = a2_ref[...].astype(BF16)
        r2 = _row_scale(ssq2_ref, d_model)
        gate2 = jnp.dot(a2, wg_ref[...].astype(BF16), preferred_element_type=F32) * r2
        up2 = jnp.dot(a2, wu_ref[...].astype(BF16), preferred_element_type=F32) * r2
        conv2 = bc_ref[...] + wc_ref[2:3, :] * gate2 + wc_ref[1:2, :] * st2_ref[1] + wc_ref[0:1, :] * st2_ref[0]
        o2_ref[...] = (_silu(conv2) * up2).astype(o2_ref.dtype)
        st2o_ref[0] = st2_ref[1]
        st2o_ref[1] = gate2

    a = a_ref[...]
    r = _row_scale(ssq_ref, d_model)
    gate = jnp.dot(a, wg_ref[...].astype(BF16), preferred_element_type=F32) * r
    up = jnp.dot(a, wu_ref[...].astype(BF16), preferred_element_type=F32) * r
    gp_ref[8:8 + tm, :] = gate
    conv = bc_ref[...] + wc_ref[2:3, :] * gate
    conv = conv + wc_ref[1:2, :] * gp_ref[7:7 + tm, :]
    conv = conv + wc_ref[0:1, :] * gp_ref[6:6 + tm, :]
    o_ref[...] = (_silu(conv) * up).astype(o_ref.dtype)
    st_ref[0] = gp_ref[tm + 6:tm + 8, :]
    gp_ref[0:8, :] = gp_ref[tm:tm + 8, :]


def ffn_up(xg, row_ssq, bsz, t_len, xg2, row_ssq2, state2, w_up, w_conv, b_conv, layer, tm=1024, tn=256):
    m, d = xg.shape
    b2 = xg2.shape[0]
    d_ff = w_up.shape[2] // 2
    tm = min(tm, t_len)
    nj = d_ff // tn
    assert d_ff % tn == 0 and t_len % tm == 0 and tm % 8 == 0
    tps = t_len // tm
    vmem = 2 * tm * d * 2 + 4 * d * tn * 4 + 2 * d * tn * 2 + 2 * tm * tn * 2 + (tm + 8) * tn * 4 + 6 * tm * tn * 4
    kern = functools.partial(_ffn_up_kernel, tiles_per_seq=tps, tm=tm, d_model=d)
    return pl.pallas_call(
        kern,
        grid=(nj, m // tm),
        in_specs=[pl.BlockSpec((tm, d), lambda j, i: (i, 0)),
                  pl.BlockSpec((None, d, tn), lambda j, i: (layer, 0, j)),
                  pl.BlockSpec((None, d, tn), lambda j, i: (layer, 0, j + nj)),
                  pl.BlockSpec((tm, LANES), lambda j, i: (i, 0)),
                  pl.BlockSpec((None, FFN_CONV, tn), lambda j, i: (layer, 0, j)),
                  pl.BlockSpec((None, 1, tn), lambda j, i: (layer, 0, j)),
                  pl.BlockSpec((b2, d), lambda j, i: (0, 0)),
                  pl.BlockSpec((b2, LANES), lambda j, i: (0, 0)),
                  pl.BlockSpec((FFN_CONV - 1, b2, tn), lambda j, i: (0, 0, j))],
        out_specs=[pl.BlockSpec((tm, tn), lambda j, i: (i, j)),
                   pl.BlockSpec((1, FFN_CONV - 1, tn), lambda j, i: (i // tps, 0, j)),
                   pl.BlockSpec((b2, tn), lambda j, i: (0, j)),
                   pl.BlockSpec((FFN_CONV - 1, b2, tn), lambda j, i: (0, 0, j))],
        out_shape=[jax.ShapeDtypeStruct((m, d_ff), BF16),
                   jax.ShapeDtypeStruct((bsz, FFN_CONV - 1, d_ff), F32),
                   jax.ShapeDtypeStruct((b2, d_ff), F32),
                   jax.ShapeDtypeStruct((FFN_CONV - 1, b2, d_ff), F32)],
        scratch_shapes=[pltpu.VMEM((tm + 8, tn), F32)],
        compiler_params=_cparams(("parallel", "arbitrary"), vmem),
        name="ffn_up",
    )(xg, w_up, w_up, row_ssq, w_conv.astype(F32), b_conv.reshape(-1, 1, d_ff).astype(F32),
      xg2, row_ssq2, state2)


def _dil_attn_kernel(q0_ref, q1_ref, q2_ref, k0_ref, k1_ref, k2_ref, v0_ref, v1_ref, v2_ref,
                     o_ref, m_ref, l_ref, acc_ref, *, t_len):
    q_refs, k_refs, v_refs = (q0_ref, q1_ref, q2_ref), (k0_ref, k1_ref, k2_ref), (v0_ref, v1_ref, v2_ref)
    scale = HEAD_DIM ** -0.5
    blk = 128
    ii = lax.broadcasted_iota(jnp.int32, (blk, blk), 0)
    jj = lax.broadcasted_iota(jnp.int32, (blk, blk), 1)
    cur_ok = jj <= ii
    prev_ok = jj >= ii
    nt = (((1,), (1,)), ((), ()))
    for g, (w, r) in enumerate(DIL_PATTERNS):
        assert w // r == blk
        n_blk = t_len // (r * blk)
        for cls in range(r):
            for qb in range(n_blk):
                rows = pl.ds(cls + qb * blk * r, blk, stride=r) if r > 1 else pl.ds(qb * blk, blk)
                q = q_refs[g][rows, :].astype(BF16)
                k = k_refs[g][rows, :].astype(BF16)
                v = v_refs[g][rows, :].astype(BF16)
                s = lax.dot_general(q, k, nt, preferred_element_type=F32) * scale
                s = jnp.where(cur_ok, s, -jnp.inf)
                m = jnp.max(s, axis=-1, keepdims=True)
                if qb > 0:
                    prow = (pl.ds(cls + (qb - 1) * blk * r, blk, stride=r) if r > 1
                            else pl.ds((qb - 1) * blk, blk))
                    kp = k_refs[g][prow, :].astype(BF16)
                    vp = v_refs[g][prow, :].astype(BF16)
                    sp = lax.dot_general(q, kp, nt, preferred_element_type=F32) * scale
                    sp = jnp.where(prev_ok, sp, -jnp.inf)
                    m = jnp.maximum(m, jnp.max(sp, axis=-1, keepdims=True))
                p = jnp.exp(s - m)
                den = jnp.sum(p, axis=-1, keepdims=True)
                acc = jnp.dot(p.astype(BF16), v, preferred_element_type=F32)
                if qb > 0:
                    pp = jnp.exp(sp - m)
                    den = den + jnp.sum(pp, axis=-1, keepdims=True)
                    acc = acc + jnp.dot(pp.astype(BF16), vp, preferred_element_type=F32)
                m = jnp.broadcast_to(m, (blk, HEAD_DIM))
                den = jnp.broadcast_to(den, (blk, HEAD_DIM))
                if g == 0:
                    m_ref[rows, :] = m
                    l_ref[rows, :] = den
                    acc_ref[rows, :] = acc
                else:
                    m_old = m_ref[rows, :]
                    m_new = jnp.maximum(m_old, m)
                    a_old = jnp.exp(m_old - m_new)
                    a_new = jnp.exp(m - m_new)
                    m_ref[rows, :] = m_new
                    l_ref[rows, :] = l_ref[rows, :] * a_old + den * a_new
                    acc_ref[rows, :] = acc_ref[rows, :] * a_old + acc * a_new
    o_ref[...] = (acc_ref[...] / l_ref[...]).astype(o_ref.dtype)


def dilated_attention_prompt(qq, kv, bsz, t_len):
    nh = DIL_HEADS
    tspec = lambda colfn: pl.BlockSpec((t_len, HEAD_DIM), lambda b, h: (b, colfn(h)))
    in_specs = ([tspec(lambda h, g=g: g * nh + h) for g in range(3)]
                + [tspec(lambda h, g=g: g * 2 * nh + h) for g in range(3)]
                + [tspec(lambda h, g=g: g * 2 * nh + nh + h) for g in range(3)])
    vmem = 2 * 9 * t_len * HEAD_DIM * 4 + 2 * t_len * HEAD_DIM * 2 + 3 * t_len * HEAD_DIM * 4 + (8 << 20)
    return pl.pallas_call(
        functools.partial(_dil_attn_kernel, t_len=t_len),
        grid=(bsz, nh),
        in_specs=in_specs,
        out_specs=pl.BlockSpec((t_len, HEAD_DIM), lambda b, h: (b, h)),
        out_shape=jax.ShapeDtypeStruct((bsz * t_len, nh * HEAD_DIM), BF16),
        scratch_shapes=[pltpu.VMEM((t_len, HEAD_DIM), F32)] * 3,
        compiler_params=_cparams(("parallel", "parallel"), vmem),
        name="dilated_attention",
    )(qq, qq, qq, kv, kv, kv, kv, kv, kv)


def _dil_step_kernel(q_ref, kvn_ref, c0_ref, c1_ref, c2_ref, o_ref):
    caches = (c0_ref, c1_ref, c2_ref)
    scale = HEAD_DIM ** -0.5
    nh = DIL_HEADS
    s_past, s_new = [], []
    m = None
    for g in range(3):
        q = q_ref[0, g]
        s = jnp.sum(caches[g][0, :, 0:nh, :] * q[None], axis=-1, keepdims=True) * scale
        sn = jnp.sum(kvn_ref[0, g, 0] * q, axis=-1, keepdims=True) * scale
        s_past.append(s)
        s_new.append(sn)
        mg = jnp.maximum(jnp.max(s, axis=0), sn)
        m = mg if m is None else jnp.maximum(m, mg)
    den = jnp.zeros((nh, 1), F32)
    acc = jnp.zeros((nh, HEAD_DIM), F32)
    for g in range(3):
        p = jnp.exp(s_past[g] - m[None])
        pn = jnp.exp(s_new[g] - m)
        den = den + jnp.sum(p, axis=0) + pn
        acc = acc + jnp.sum(p * caches[g][0, :, nh:2 * nh, :], axis=0) + pn * kvn_ref[0, g, 1]
    o_ref[0] = (acc / den).astype(o_ref.dtype)


def dilated_attention_step(q, kv_new, caches):
    bsz = q.shape[0]
    nh = DIL_HEADS
    views = []
    for cache, (w, r) in zip(caches, DIL_PATTERNS):
        assert cache.shape[1] == w, "rolling window cache must hold the full window"
        views.append(cache.reshape(bsz, w // r, r * 2 * nh, HEAD_DIM))
    nkeys = DIL_PATTERNS[0][0] // DIL_PATTERNS[0][1]
    return pl.pallas_call(
        _dil_step_kernel,
        grid=(bsz,),
        in_specs=[pl.BlockSpec((1, 3, nh, HEAD_DIM), lambda b: (b, 0, 0, 0)),
                  pl.BlockSpec((1, 3, 2, nh, HEAD_DIM), lambda b: (b, 0, 0, 0, 0))]
                 + [pl.BlockSpec((1, nkeys, 2 * nh, HEAD_DIM), lambda b: (b, 0, 0, 0)) for _ in DIL_PATTERNS],
        out_specs=pl.BlockSpec((1, nh, HEAD_DIM), lambda b: (b, 0, 0)),
        out_shape=jax.ShapeDtypeStruct((bsz, nh, HEAD_DIM), F32),
        compiler_params=_cparams(("parallel",), 2 * 3 * nkeys * 2 * nh * HEAD_DIM * 4 + (8 << 20)),
        name="dilated_attention_step",
    )(q, kv_new, *views)


def _mem_step_kernel(q_ref, kv_ref, o_ref):
    q = q_ref[0]
    s = jnp.sum(kv_ref[0, :, 0] * q[None], axis=-1, keepdims=True) * (MEM_HEAD_DIM ** -0.5)
    p = jnp.exp(s - jnp.max(s, axis=0)[None])
    o_ref[0] = (jnp.sum(p * kv_ref[0, :, 1], axis=0) / jnp.sum(p, axis=0)).astype(o_ref.dtype)


def mem_attention_step(q, cache, layer):
    bsz = q.shape[0]
    return pl.pallas_call(
        _mem_step_kernel,
        grid=(bsz,),
        in_specs=[pl.BlockSpec((1, MEM_HEADS, MEM_HEAD_DIM), lambda b: (b, 0, 0)),
                  pl.BlockSpec((None, 1, N_MEM, 2, MEM_HEADS, MEM_HEAD_DIM), lambda b: (layer, b, 0, 0, 0, 0))],
        out_specs=pl.BlockSpec((1, MEM_HEADS, MEM_HEAD_DIM), lambda b: (b, 0, 0)),
        out_shape=jax.ShapeDtypeStruct((bsz, MEM_HEADS, MEM_HEAD_DIM), F32),
        compiler_params=_cparams(("parallel",), 4 * N_MEM * 2 * 8 * MEM_HEAD_DIM * 4 + (8 << 20)),
        name="mem_attention_step",
    )(q, cache)


class _Group:
    def __init__(self, x, bsz, t_len, pos, mem_kv, ssm_prev=None, conv_prev=None, win_past=None):
        self.x, self.bsz, self.t_len, self.mem_kv = x, bsz, t_len, mem_kv
        self.ssm_prev, self.conv_prev, self.win_past = ssm_prev, conv_prev, win_past
        self.prompt = ssm_prev is None
        m = x.shape[0]
        self.rope = (rope_tables(pos) if self.prompt
                     else tuple(jnp.broadcast_to(t, (m, HEAD_DIM)) for t in rope_tables(pos)))

    def memory_attention(self, q, q_colblk, layer):
        m = q.shape[0]
        if self.prompt:
            return mem_attention(q.reshape(self.bsz, self.t_len, -1), q_colblk, self.mem_kv[layer]).reshape(m, MEM_W)
        q = q[:, q_colblk * MEM_W:(q_colblk + 1) * MEM_W].reshape(m, MEM_HEADS, MEM_HEAD_DIM)
        return mem_attention_step(q, self.mem_kv, layer).reshape(m, MEM_W)


def _mixer_a(g, p):
    m, d = g.x.shape
    wt_in = jnp.swapaxes(p["w_in_a"], 1, 2)
    zx_cols = SSM_INNER + SSM_XBC
    h = rmsnorm_rows(g.x, p["g_mix"][0])
    zxbc = matmul([h], wt_in, 0, n=zx_cols, w_t=True, name="in_proj_a")
    dt_raw = matmul([h], wt_in, 0, n=LANES, w_col0=zx_cols, tn=LANES, w_t=True, name="in_proj_dt")
    qm = matmul([h], wt_in[:, zx_cols + SSM_HEADS:], 0, n=MEM_W, w_t=True, modes=["norm256"] * 2,
                gain=jnp.tile(p["g_mem_q"][0], MEM_HEADS), name="in_proj_qmem")
    y_mem = g.memory_attention(qm, 0, 0)
    ssm_args = (p["w_conv_a"][0], p["b_conv_a"][0], p["dt_bias_a"][0], p["a_log_a"][0], p["d_skip_a"][0],
                p["g_ssm_out_a"][0])
    if g.prompt:
        u, g.ssm_new = ssd_mixer(zxbc, dt_raw, g.bsz, g.t_len, *ssm_args)
        g.conv_new = zxbc.reshape(g.bsz, g.t_len, zx_cols)[:, g.t_len - (SSM_CONV - 1):, SSM_INNER:]
    else:
        u, g.ssm_new, g.conv_new = ssd_step(zxbc, dt_raw, g.ssm_prev[0], g.conv_prev[0], *ssm_args)
    g.x, g.xg, g.ssq = matmul([u, y_mem], p["w_out_a"], 0, n=d, res=g.x, emit_gains=[p["g_ffn"][0]],
                              name="out_proj_a")


def _mixer_b(g, xg_kv, xg_mix, p):
    m, d = g.x.shape
    nh = DIL_HEADS
    gk = jnp.concatenate([jnp.concatenate([jnp.tile(p["g_k_dil"][i], nh), jnp.ones((nh * HEAD_DIM,), F32)])
                          for i in range(3)])
    g.kv = matmul([xg_kv], p["w_kv"][None], 0, n=6 * nh * HEAD_DIM, row_ssq=(g.ssq, d),
                  modes=(["norm128rope"] * 2 + ["plain"] * 2) * 3, gain=gk, rope=g.rope, name="kv_proj")
    gq = jnp.concatenate([jnp.tile(p["g_q_dil"][0][i], nh) for i in range(3)]
                         + [jnp.tile(p["g_mem_q"][1], MEM_HEADS)])
    qq = matmul([xg_mix], p["w_in_b"], 0, n=d, row_ssq=(g.ssq, d), modes=["norm128rope"] * 6 + ["norm256"] * 2,
                gain=gq, rope=g.rope, name="in_proj_b")
    y_mem = g.memory_attention(qq, 3, 1)
    if g.prompt:
        att = dilated_attention_prompt(qq, g.kv, g.bsz, g.t_len)
    else:
        att = dilated_attention_step(qq[:, :3 * nh * HEAD_DIM].reshape(m, 3, nh, HEAD_DIM),
                                     g.kv.reshape(m, 3, 2, nh, HEAD_DIM), g.win_past).reshape(m, nh * HEAD_DIM)
    g.x, g.xg, g.ssq = matmul([att, y_mem], p["w_out_b"], 0, n=d, res=g.x, emit_gains=[p["g_ffn"][1]],
                              name="out_proj_b")


def _conv_ffn(gp, gs, ffn_prev, layer, p, emit_gains=()):
    state_s = jnp.swapaxes(ffn_prev[layer].astype(F32), 0, 1)
    a_p, st_p, a_s, st_s = ffn_up(gp.xg, gp.ssq, gp.bsz, gp.t_len, gs.xg, gs.ssq, state_s,
                                  p["w_ffn_up"], p["w_ffn_conv"], p["b_ffn_conv"], layer)
    out_p, out_s = matmul_long_k(a_p, p["w_ffn_down"], layer, gp.x, a_s, gs.x, emit_gains=emit_gains,
                                 name="ffn_down")
    return out_p, out_s, st_p, jnp.swapaxes(st_s, 0, 1)


def _trunk(gp, gs, ffn_prev, p):
    _mixer_a(gp, p)
    _mixer_a(gs, p)
    out_p, out_s, st0_p, st0_s = _conv_ffn(gp, gs, ffn_prev, 0, p, emit_gains=[p["g_kv"], p["g_mix"][1]])
    for g, (x, xg_kv, xg_mix, ssq) in ((gp, out_p), (gs, out_s)):
        g.x, g.ssq = x, ssq
        _mixer_b(g, xg_kv, xg_mix, p)
    out_p, out_s, st1_p, st1_s = _conv_ffn(gp, gs, ffn_prev, 1, p)
    return out_p[0], out_s[0], jnp.stack([st0_p, st1_p], axis=0), jnp.stack([st0_s, st1_s], axis=0)


def _memory_kv(mem, g_norm, w_kv, layer, g_k):
    bsz, n, d = mem.shape
    hm = rmsnorm_rows(mem.reshape(bsz * n, d), g_norm)
    gain = jnp.concatenate([jnp.tile(g_k, MEM_HEADS), jnp.ones((MEM_W,), F32)])
    kv = matmul([hm], w_kv, layer, n=2 * MEM_W, modes=["norm256"] * 2 + ["plain"] * 2, gain=gain,
                name="mem_kv_proj")
    return kv.reshape(bsz, n, 2 * MEM_W)


def kernel(x_prompt, x_sample, state_ssm, state_ssm_conv, state_ffn_conv, cache_mem_kv, cache_win_kv0, cache_win_kv1, cache_win_kv2, mem_prompt, g_mix, w_in_a, w_conv_a, b_conv_a, dt_bias_a, a_log_a, d_skip_a, g_ssm_out_a, w_out_a, g_kv, w_kv, g_k_dil, w_in_b, g_q_dil, w_out_b, g_mem, w_mem_kv, g_mem_q, g_mem_k, g_ffn, w_ffn_up, w_ffn_conv, b_ffn_conv, w_ffn_down):
    p = dict(g_mix=g_mix, w_in_a=w_in_a, w_conv_a=w_conv_a, b_conv_a=b_conv_a, dt_bias_a=dt_bias_a,
             a_log_a=a_log_a, d_skip_a=d_skip_a, g_ssm_out_a=g_ssm_out_a, w_out_a=w_out_a, g_kv=g_kv, w_kv=w_kv,
             g_k_dil=g_k_dil, w_in_b=w_in_b, g_q_dil=g_q_dil, w_out_b=w_out_b, g_mem_q=g_mem_q, g_ffn=g_ffn,
             w_ffn_up=w_ffn_up, w_ffn_conv=w_ffn_conv, b_ffn_conv=b_ffn_conv, w_ffn_down=w_ffn_down)
    bp, sp, d = x_prompt.shape
    bs, ds, _ = x_sample.shape
    depth = g_mix.shape[0]
    nh = DIL_HEADS

    mem_kv_p = [_memory_kv(mem_prompt, g_mem[i], w_mem_kv, i, g_mem_k[i]) for i in range(depth)]
    gp = _Group(x_prompt.reshape(bp * sp, d), bp, sp, jnp.arange(sp, dtype=jnp.int32), mem_kv_p)
    assert ds == 1
    gs = _Group(x_sample.reshape(bs * ds, d), bs, ds, PAST_LEN + jnp.arange(ds, dtype=jnp.int32), cache_mem_kv,
                state_ssm, state_ssm_conv, [cache_win_kv0, cache_win_kv1, cache_win_kv2])
    y_p, y_s, ffn_p, ffn_s = _trunk(gp, gs, state_ffn_conv, p)

    kv_p = gp.kv.reshape(bp, sp, 3, 2, nh, HEAD_DIM)
    kv_s = gs.kv.reshape(bs, ds, 3, 2, nh, HEAD_DIM)
    win_p = [kv_p[:, sp - min(w, sp):, g] for g, (w, _) in enumerate(DIL_PATTERNS)]
    mem_out = jnp.stack(mem_kv_p, axis=0).reshape(depth, bp, N_MEM, 2, MEM_HEADS, MEM_HEAD_DIM)
    return (y_p.reshape(bp, sp, d), y_s.reshape(bs, ds, d), gp.ssm_new[None], gs.ssm_new[None],
            gp.conv_new[None], gs.conv_new[None], ffn_p, ffn_s, mem_out, win_p[0], win_p[1], win_p[2],
            kv_s[:, :, 0], kv_s[:, :, 1], kv_s[:, :, 2])
```

```python
import functools
import math

import jax
import jax.numpy as jnp
from jax import lax
from jax.experimental import pallas as pl
from jax.experimental.pallas import tpu as pltpu

F32 = jnp.float32
BF16 = jnp.bfloat16
EPS = 1e-6
LANES = 128
V7X_VMEM_BYTES = 64 * 2**20

SSM_HEAD_DIM = 64
SSM_HEADS = 48
SSM_GROUPS = 8
SSM_STATE = 128
SSM_INNER = SSM_HEADS * SSM_HEAD_DIM
SSM_BC = SSM_GROUPS * SSM_STATE
SSM_XBC = SSM_INNER + 2 * SSM_BC
SSM_CONV = 4
SSM_CHUNK = 128
SSM_PAIRS = SSM_HEADS // 2
DIL_PATTERNS = ((128, 1), (512, 4), (2048, 16))
DIL_HEADS = 8
HEAD_DIM = 128
ROT_DIM = HEAD_DIM // 4
ROPE_THETA = 500000.0
N_MEM = 256
MEM_HEADS = 4
MEM_HEAD_DIM = 256
MEM_W = MEM_HEADS * MEM_HEAD_DIM
FFN_CONV = 3
PAST_LEN = 16384


def _cparams(sem, vmem_bytes):
    limit = min(int(vmem_bytes * 1.25) + (4 << 20), V7X_VMEM_BYTES - (6 << 20))
    return pltpu.CompilerParams(dimension_semantics=sem or None, vmem_limit_bytes=limit)


def _silu(x):
    h = 0.5 * x
    return h + h * jnp.tanh(h)


def _softplus(x):
    return jnp.maximum(x, 0.0) + jnp.log1p(jnp.exp(-jnp.abs(x)))


def _rmsnorm_kernel(x_ref, g_ref, o_ref):
    x = x_ref[...]
    ms = jnp.mean(x * x, axis=-1, keepdims=True)
    o_ref[...] = (x * lax.rsqrt(ms + EPS) * g_ref[...]).astype(o_ref.dtype)


def rmsnorm_rows(x, g, tm=256):
    m, d = x.shape
    tm = min(tm, m)
    return pl.pallas_call(
        _rmsnorm_kernel,
        grid=(m // tm,),
        in_specs=[pl.BlockSpec((tm, d), lambda i: (i, 0)),
                  pl.BlockSpec((1, d), lambda i: (0, 0))],
        out_specs=pl.BlockSpec((tm, d), lambda i: (i, 0)),
        out_shape=jax.ShapeDtypeStruct((m, d), BF16),
        compiler_params=_cparams(("parallel",), 2 * tm * d * 6),
        name="rmsnorm_rows",
    )(x, g.reshape(1, d).astype(F32))


def rope_tables(pos):
    half = ROT_DIM // 2
    inv_freq = jnp.exp(-(2.0 * jnp.arange(half, dtype=F32) / ROT_DIM) * math.log(ROPE_THETA))
    ang = pos.astype(F32)[:, None] * inv_freq[None, :]
    cos, sin = jnp.cos(ang), jnp.sin(ang)
    n = pos.shape[0]
    ones = jnp.ones((n, HEAD_DIM - ROT_DIM), F32)
    zeros = jnp.zeros((n, HEAD_DIM - ROT_DIM), F32)
    zh = jnp.zeros((n, half), F32)
    c = jnp.concatenate([cos, cos, ones], axis=1)
    s_lo = jnp.concatenate([-sin, zh, zeros], axis=1)
    s_hi = jnp.concatenate([zh, sin, zeros], axis=1)
    return c, s_lo, s_hi


def _head_norm(blk, gain, hd):
    ms = jnp.sum(blk * blk, axis=-1, keepdims=True) * (1.0 / hd)
    return blk * lax.rsqrt(ms + EPS) * gain


def _row_scale(ssq_ref, d_model):
    return lax.rsqrt(jnp.sum(ssq_ref[...], axis=-1, keepdims=True) * (1.0 / d_model) + EPS)


def _lane_partial_ssq(y):
    y2 = y * y
    part = y2[:, 0:LANES]
    for c0 in range(LANES, y.shape[1], LANES):
        part = part + y2[:, c0:c0 + LANES]
    return part


class _Epi:
    def __init__(self, it, *, d_scale, has_gain, has_rope, has_res):
        self.ssq_in = next(it) if d_scale else None
        self.gain = next(it) if has_gain else None
        self.rope = (next(it), next(it), next(it)) if has_rope else None
        self.res = next(it) if has_res else None

    def take_outputs(self, it, n_emit):
        self.out = next(it)
        self.emits = [next(it) for _ in range(n_emit)]
        self.ssq_out = next(it) if n_emit else None


def _mm_kernel(*refs, n_a, modes, nj, n_tiles, tn, w_t, has_gain, has_rope, has_res, d_scale, n_emit,
               pipelined, has_rider):
    it = iter(refs)
    a_refs = [next(it) for _ in range(n_a)]
    w_refs = [next(it) for _ in range(n_a)]
    flags = dict(d_scale=d_scale, has_gain=has_gain, has_rope=has_rope, has_res=has_res)
    main = _Epi(it, **flags)
    emit_gain_refs = [next(it) for _ in range(n_emit)]
    if has_rider:
        a2_refs = [next(it) for _ in range(n_a)]
        rider = _Epi(it, **dict(flags, has_gain=has_gain and pipelined))
        if has_gain and not pipelined:
            rider.gain = main.gain
    main.take_outputs(it, n_emit)
    if has_rider:
        rider.take_outputs(it, n_emit)
    p_bufs = (next(it), next(it)) if pipelined else None
    s = pl.program_id(0)
    j_epi = (jnp.maximum(s - 1, 0) if pipelined else s) % nj

    def product(lhs_refs):
        p = None
        for a_ref, w_ref in zip(lhs_refs, w_refs):
            wb = w_ref[...].astype(BF16)
            dims = (((1,), (1,)), ((), ())) if w_t else (((1,), (0,)), ((), ()))
            d = lax.dot_general(a_ref[...].astype(BF16), wb, dims, preferred_element_type=F32)
            p = d if p is None else p + d
        return p

    def epilogue(mode, p, e, j):
        if d_scale:
            p = p * _row_scale(e.ssq_in, d_scale)
        if mode == "plain":
            y = p + e.res[...] if has_res else p
            e.out[...] = y.astype(e.out.dtype)
            if n_emit:
                for g_ref, e_ref in zip(emit_gain_refs, e.emits):
                    e_ref[...] = (y * g_ref[...]).astype(e_ref.dtype)
                part = _lane_partial_ssq(y)

                @pl.when(j == 0)
                def _():
                    e.ssq_out[...] = part

                @pl.when(j > 0)
                def _():
                    e.ssq_out[...] += part
            return
        hd = 256 if mode == "norm256" else 128
        for h0 in range(0, tn, hd):
            y = _head_norm(p[:, h0:h0 + hd], e.gain[:, h0:h0 + hd], hd)
            if mode == "norm128rope":
                c_ref, slo_ref, shi_ref = e.rope
                half = ROT_DIM // 2
                y = (y * c_ref[...]
                     + pltpu.roll(y, hd - half, 1) * slo_ref[...]
                     + pltpu.roll(y, half, 1) * shi_ref[...])
            e.out[:, h0:h0 + hd] = y.astype(e.out.dtype)

    def stage(mode, parity):
        if pipelined:
            new, old = (p_bufs[0], p_bufs[1]) if parity == 0 else (p_bufs[1], p_bufs[0])
            new[...] = product(a_refs)
            epilogue(mode, old[...], main, j_epi)
        else:
            epilogue(mode, product(a_refs), main, j_epi)

    if pipelined:
        @pl.when(s == 0)
        def _():
            p_bufs[1][...] = jnp.zeros_like(p_bufs[1])

    distinct = sorted(set(modes))

    def mode_cond(mode, j):
        cond = None
        if len(distinct) > 1:
            for jj, mj in enumerate(modes):
                if mj == mode:
                    c = j == jj
                    cond = c if cond is None else jnp.logical_or(cond, c)
        return cond

    for mode in distinct:
        cond = mode_cond(mode, j_epi)
        for parity in ((0, 1) if pipelined else (None,)):
            c = cond
            if parity is not None:
                cp = (s % 2) == parity
                c = cp if c is None else jnp.logical_and(c, cp)
            if c is None:
                stage(mode, parity)
            else:
                pl.when(c)(functools.partial(stage, mode, parity))

    if has_rider:
        for mode in distinct:
            c = mode_cond(mode, s)
            c = s < nj if c is None else jnp.logical_and(c, s < nj)
            pl.when(c)(lambda mode=mode: epilogue(mode, product(a2_refs), rider, s))


def matmul(a_list, w, layer, *, n, tm=1024, tn=512, w_col0=0, w_t=False, out_dtype=F32, modes=None,
           gain=None, rope=None, res=None, row_ssq=None, emit_gains=(), rider=None, name="matmul"):
    m = a_list[0].shape[0]
    tm = min(tm, m)
    assert m % tm == 0 and n % tn == 0 and w_col0 % tn == 0
    nj = n // tn
    n_tiles = (m // tm) * nj
    modes = tuple(modes) if modes is not None else ("plain",) * nj
    assert len(modes) == nj
    pipelined = n_tiles >= 16 and "norm128rope" in modes
    cb = w_col0 // tn
    cur = (lambda s: jnp.minimum(s, n_tiles - 1)) if pipelined else (lambda s: s)
    epi = (lambda s: jnp.maximum(s - 1, 0)) if pipelined else (lambda s: s)
    in_specs, w_specs, vmem, row0 = [], [], 0, 0
    for a in a_list:
        ki = a.shape[1]
        assert row0 % ki == 0
        rb = row0 // ki
        in_specs.append(pl.BlockSpec((tm, ki), lambda s: (cur(s) // nj, 0), pipeline_mode=pl.Buffered(1)))
        if w_t:
            w_specs.append(pl.BlockSpec((None, tn, ki), lambda s, rb=rb: (layer, cur(s) % nj + cb, rb)))
        else:
            w_specs.append(pl.BlockSpec((None, ki, tn), lambda s, rb=rb: (layer, rb, cur(s) % nj + cb)))
        vmem += tm * ki * a.dtype.itemsize + tm * ki * 2 + 2 * ki * tn * 4 + ki * tn * 2
        row0 += ki
    assert row0 == w.shape[2 if w_t else 1]
    in_specs += w_specs
    args = list(a_list) + [w] * len(a_list)
    vmem += 2 * tm * tn * jnp.dtype(out_dtype).itemsize + 4 * tm * tn * 4
    tile_spec = pl.BlockSpec((tm, tn), lambda s: (epi(s) // nj, epi(s) % nj))
    col_spec = pl.BlockSpec((1, tn), lambda s: (0, epi(s) % nj))
    ssq_spec = pl.BlockSpec((tm, LANES), lambda s: (epi(s) // nj, 0))
    if row_ssq is not None:
        in_specs.append(ssq_spec)
        args.append(row_ssq[0])
    if gain is not None:
        in_specs.append(col_spec)
        args.append(gain.reshape(1, n).astype(F32))
    if rope is not None:
        period = rope[0].shape[0]
        assert period % tm == 0
        nper = period // tm
        for t in rope:
            in_specs.append(pl.BlockSpec((tm, HEAD_DIM), lambda s: ((epi(s) // nj) % nper, 0)))
            args.append(t)
    if res is not None:
        in_specs.append(tile_spec)
        args.append(res)
        vmem += 2 * tm * tn * 4
    out_specs = [tile_spec]
    out_shape = [jax.ShapeDtypeStruct((m, n), out_dtype)]
    if emit_gains:
        assert nj * tn == n and all(g.shape == (n,) for g in emit_gains)
        for g in emit_gains:
            in_specs.append(col_spec)
            args.append(g.reshape(1, n).astype(F32))
            out_specs.append(tile_spec)
            out_shape.append(jax.ShapeDtypeStruct((m, n), BF16))
            vmem += 2 * tm * tn * 2
        out_specs.append(ssq_spec)
        out_shape.append(jax.ShapeDtypeStruct((m, LANES), F32))
    n_main_out = len(out_specs)
    if rider is not None:
        assert m // tm > 1
        b2 = rider["a_list"][0].shape[0]
        park = lambda s: jnp.minimum(s, nj - 1)
        whole = lambda width: pl.BlockSpec((b2, width), lambda s: (0, 0))
        tile2_spec = pl.BlockSpec((b2, tn), lambda s: (0, park(s)))
        for a2 in rider["a_list"]:
            in_specs.append(whole(a2.shape[1]))
            args.append(a2)
        if row_ssq is not None:
            in_specs.append(whole(LANES))
            args.append(rider["row_ssq"])
        if gain is not None and pipelined:
            in_specs.append(pl.BlockSpec((1, tn), lambda s: (0, park(s))))
            args.append(gain.reshape(1, n).astype(F32))
        if rope is not None:
            for t in rider["rope"]:
                in_specs.append(whole(HEAD_DIM))
                args.append(t)
        if res is not None:
            in_specs.append(tile2_spec)
            args.append(rider["res"])
        out_specs.append(tile2_spec)
        out_shape.append(jax.ShapeDtypeStruct((b2, n), F32))
        for _ in emit_gains:
            out_specs.append(tile2_spec)
            out_shape.append(jax.ShapeDtypeStruct((b2, n), F32))
        if emit_gains:
            out_specs.append(whole(LANES))
            out_shape.append(jax.ShapeDtypeStruct((b2, LANES), F32))
    kern = functools.partial(_mm_kernel, n_a=len(a_list), modes=modes, nj=nj, n_tiles=n_tiles, tn=tn, w_t=w_t,
                             has_gain=gain is not None, has_rope=rope is not None, has_res=res is not None,
                             d_scale=row_ssq[1] if row_ssq is not None else 0, n_emit=len(emit_gains),
                             pipelined=pipelined, has_rider=rider is not None)
    outs = pl.pallas_call(
        kern,
        grid=(n_tiles + 1 if pipelined else n_tiles,),
        in_specs=in_specs,
        out_specs=out_specs,
        out_shape=out_shape,
        scratch_shapes=[pltpu.VMEM((tm, tn), F32)] * 2 if pipelined else [],
        compiler_params=_cparams(("arbitrary",), vmem),
        name=name,
    )(*args)
    pick = lambda group: tuple(group) if emit_gains else group[0]
    if rider is None:
        return pick(outs)
    return pick(outs[:n_main_out]), pick(outs[n_main_out:])


def _mm_panel_kernel(*refs, nk, n_emit):
    it = iter(refs)
    a_ref, w_ref, res_ref, a2_ref, res2_ref = (next(it) for _ in range(5))
    emit_gain_refs = [next(it) for _ in range(n_emit)]
    o_ref = next(it)
    emit_refs = [next(it) for _ in range(n_emit)]
    ssq_out_ref = next(it) if n_emit else None
    o2_ref = next(it)
    emit2_refs = [next(it) for _ in range(n_emit)]
    ssq2_out_ref = next(it) if n_emit else None
    acc_ref, acc2_ref = next(it), next(it)
    i, k, j = pl.program_id(0), pl.program_id(1), pl.program_id(2)

    def k_pass(p, acc, finish):
        @pl.when(k == 0)
        def _():
            acc[j] = p

        if nk > 2:
            @pl.when(jnp.logical_and(k > 0, k < nk - 1))
            def _():
                acc[j] += p

        @pl.when(k == nk - 1)
        def _():
            finish(acc[j] + p)

    def finisher(res, out, emits, ssq_out):
        def finish(total):
            y = total + res[...]
            out[...] = y.astype(out.dtype)
            if n_emit:
                for g_ref, e_ref in zip(emit_gain_refs, emits):
                    e_ref[...] = (y * g_ref[...]).astype(e_ref.dtype)
                part = _lane_partial_ssq(y)

                @pl.when(j == 0)
                def _():
                    ssq_out[...] = part

                @pl.when(j > 0)
                def _():
                    ssq_out[...] += part
        return finish

    k_pass(jnp.dot(a_ref[...].astype(BF16), w_ref[...].astype(BF16), preferred_element_type=F32), acc_ref,
           finisher(res_ref, o_ref, emit_refs, ssq_out_ref))

    @pl.when(i == 0)
    def _():
        k_pass(jnp.dot(a2_ref[...].astype(BF16), w_ref[...].astype(BF16), preferred_element_type=F32), acc2_ref,
               finisher(res2_ref, o2_ref, emit2_refs, ssq2_out_ref))


def matmul_long_k(a, w, layer, res, a2, res2, *, tm=1024, tn=256, nk=2, emit_gains=(), name="matmul_long_k"):
    m, kdim = a.shape
    b2 = a2.shape[0]
    n = w.shape[2]
    tm = min(tm, m)
    assert kdim % nk == 0 and (kdim // nk) % LANES == 0 and m % tm == 0 and n % tn == 0 and nk >= 2
    tk = kdim // nk
    nj = n // tn
    last = lambda k, j: jnp.where(k == nk - 1, j, 0)
    park = lambda i, k, j: jnp.where(i == 0, last(k, j), nj - 1)
    tile_spec = pl.BlockSpec((tm, tn), lambda i, k, j: (i, last(k, j)))
    tile2_spec = pl.BlockSpec((b2, tn), lambda i, k, j: (0, park(i, k, j)))
    col_spec = pl.BlockSpec((1, tn), lambda i, k, j: (0, last(k, j)))
    vmem = tm * tk * 2 + 2 * tk * tn * 4 + tk * tn * 2 + nj * tm * tn * 4 + 5 * tm * tn * 4
    in_specs = [pl.BlockSpec((tm, tk), lambda i, k, j: (i, k), pipeline_mode=pl.Buffered(1)),
                pl.BlockSpec((None, tk, tn), lambda i, k, j: (layer, k, j)),
                tile_spec,
                pl.BlockSpec((b2, tk), lambda i, k, j: (0, k)),
                tile2_spec]
    args = [a, w, res, a2, res2]
    out_specs, out2_specs = [tile_spec], [tile2_spec]
    out_shape, out2_shape = [jax.ShapeDtypeStruct((m, n), F32)], [jax.ShapeDtypeStruct((b2, n), F32)]
    for g in emit_gains:
        in_specs.append(col_spec)
        args.append(g.reshape(1, n).astype(F32))
        out_specs.append(tile_spec)
        out_shape.append(jax.ShapeDtypeStruct((m, n), BF16))
        out2_specs.append(tile2_spec)
        out2_shape.append(jax.ShapeDtypeStruct((b2, n), F32))
        vmem += 2 * tm * tn * 2
    if emit_gains:
        out_specs.append(pl.BlockSpec((tm, LANES), lambda i, k, j: (i, 0)))
        out_shape.append(jax.ShapeDtypeStruct((m, LANES), F32))
        out2_specs.append(pl.BlockSpec((b2, LANES), lambda i, k, j: (0, 0)))
        out2_shape.append(jax.ShapeDtypeStruct((b2, LANES), F32))
    outs = pl.pallas_call(
        functools.partial(_mm_panel_kernel, nk=nk, n_emit=len(emit_gains)),
        grid=(m // tm, nk, nj),
        in_specs=in_specs,
        out_specs=out_specs + out2_specs,
        out_shape=out_shape + out2_shape,
        scratch_shapes=[pltpu.VMEM((nj, tm, tn), F32), pltpu.VMEM((nj, b2, tn), F32)],
        compiler_params=_cparams(("arbitrary", "arbitrary", "arbitrary"), vmem),
        name=name,
    )(*args)
    n1 = len(out_specs)
    return tuple(outs[:n1]), tuple(outs[n1:])


def _mem_attn_kernel(q_ref, kv_ref, o_ref):
    scale = MEM_HEAD_DIM ** -0.5
    for h in range(MEM_HEADS):
        lo = h * MEM_HEAD_DIM
        q = q_ref[0, :, lo:lo + MEM_HEAD_DIM].astype(BF16)
        tq = q.shape[0]
        if tq < 8:
            q = jnp.broadcast_to(q, (8, MEM_HEAD_DIM))
        k = kv_ref[0, :, lo:lo + MEM_HEAD_DIM].astype(BF16)
        v = kv_ref[0, :, MEM_W + lo:MEM_W + lo + MEM_HEAD_DIM].astype(BF16)
        s = lax.dot_general(q, k, (((1,), (1,)), ((), ())), preferred_element_type=F32) * scale
        p = jnp.exp(s - jnp.max(s, axis=-1, keepdims=True))
        den = jnp.sum(p, axis=-1, keepdims=True)
        o = jnp.dot(p.astype(BF16), v, preferred_element_type=F32) / den
        o_ref[0, :, lo:lo + MEM_HEAD_DIM] = o[0:tq].astype(o_ref.dtype)


def mem_attention(q, q_colblk, mem_kv, tq=512):
    b, t, _ = q.shape
    tq = min(tq, t)
    return pl.pallas_call(
        _mem_attn_kernel,
        grid=(b, t // tq),
        in_specs=[pl.BlockSpec((1, tq, MEM_W), lambda i, j: (i, j, q_colblk)),
                  pl.BlockSpec((1, N_MEM, 2 * MEM_W), lambda i, j: (i, 0, 0))],
        out_specs=pl.BlockSpec((1, tq, MEM_W), lambda i, j: (i, j, 0)),
        out_shape=jax.ShapeDtypeStruct((b, t, MEM_W), BF16),
        compiler_params=_cparams(("parallel", "parallel"), 2 * (tq * MEM_W * 6 + N_MEM * 2 * MEM_W * 4) + 8 * tq * N_MEM * 4),
        name="mem_attention",
    )(q, mem_kv)


def _split3_dot(lhs_bf16, x):
    hi = x.astype(BF16)
    r1 = x - hi.astype(F32)
    mid = r1.astype(BF16)
    lo = (r1 - mid.astype(F32)).astype(BF16)
    out = jnp.dot(lhs_bf16, hi, preferred_element_type=F32)
    out += jnp.dot(lhs_bf16, mid, preferred_element_type=F32)
    out += jnp.dot(lhs_bf16, lo, preferred_element_type=F32)
    return out


def _ssd_kernel(z_ref, x_ref, b_ref, c_ref, dt_ref, wconv_ref, bconv_ref, dtb_ref, aneg_ref,
                dskip_ref, gout_ref, u_ref, hout_ref, xp_ref, xs_ref, h_ref, y_ref):
    ck = pl.program_id(1)
    n_ck = pl.num_programs(1)
    c = SSM_CHUNK

    @pl.when(ck == 0)
    def _():
        xp_ref[0:8, :] = jnp.zeros((8, SSM_XBC), F32)
        h_ref[...] = jnp.zeros_like(h_ref)

    xp_ref[8:8 + c, 0:SSM_INNER] = x_ref[...]
    xp_ref[8:8 + c, SSM_INNER:SSM_INNER + SSM_BC] = b_ref[...]
    xp_ref[8:8 + c, SSM_INNER + SSM_BC:SSM_XBC] = c_ref[...]

    slab = 512
    for c0 in range(0, SSM_XBC, slab):
        acc = bconv_ref[:, c0:c0 + slab] + wconv_ref[3:4, c0:c0 + slab] * xp_ref[8:8 + c, c0:c0 + slab]
        for tap in range(SSM_CONV - 1):
            off = 8 - (SSM_CONV - 1 - tap)
            acc = acc + wconv_ref[tap:tap + 1, c0:c0 + slab] * xp_ref[off:off + c, c0:c0 + slab]
        xs_ref[:, c0:c0 + slab] = _silu(acc)
    xp_ref[0:8, :] = xp_ref[c:c + 8, :]

    lane = lax.broadcasted_iota(jnp.int32, (c, LANES), 1)
    row = lax.broadcasted_iota(jnp.int32, (c, LANES), 0)
    dt = jnp.where(lane < SSM_HEADS, _softplus(dt_ref[...] + dtb_ref[...]), 0.0)
    la = dt * aneg_ref[...]
    tri = row >= lane
    cum = _split3_dot(jnp.where(tri, 1.0, 0.0).astype(BF16), la)
    cum_t = cum.T
    ecum = jnp.exp(cum)
    wend = jnp.exp(cum[c - 1:c, :] - cum)
    elast_t = jnp.exp(cum_t[:, c - 1:c])
    left = lane < SSM_HEAD_DIM
    top = row < SSM_HEAD_DIM

    def pair(col0, col1):
        return jnp.where(left, col0, col1)

    for g in range(SSM_GROUPS):
        bg = xs_ref[:, SSM_INNER + g * SSM_STATE:SSM_INNER + (g + 1) * SSM_STATE].astype(BF16)
        cg = xs_ref[:, SSM_INNER + SSM_BC + g * SSM_STATE:SSM_INNER + SSM_BC + (g + 1) * SSM_STATE].astype(BF16)
        cb = lax.dot_general(cg, bg, (((1,), (1,)), ((), ())), preferred_element_type=F32)
        for q in range(SSM_PAIRS // SSM_GROUPS):
            pr = g * (SSM_PAIRS // SSM_GROUPS) + q
            h0, h1 = 2 * pr, 2 * pr + 1
            xpair = xs_ref[:, pr * LANES:(pr + 1) * LANES]
            xdt = xpair * pair(dt[:, h0:h0 + 1], dt[:, h1:h1 + 1])
            xdt_b = xdt.astype(BF16)
            ys = []
            for hh in (h0, h1):
                seg = cum[:, hh:hh + 1] - cum_t[hh:hh + 1, :]
                decay = jnp.exp(jnp.where(tri, seg, -jnp.inf))
                ys.append(jnp.dot((cb * decay).astype(BF16), xdt_b, preferred_element_type=F32))
            y = pair(ys[0], ys[1])
            hp = h_ref[pr]
            y_in = lax.dot_general(cg, hp.astype(BF16), (((1,), (1,)), ((), ())), preferred_element_type=F32)
            y = y + y_in * pair(ecum[:, h0:h0 + 1], ecum[:, h1:h1 + 1])
            xw = xdt * pair(wend[:, h0:h0 + 1], wend[:, h1:h1 + 1])
            upd = jnp.dot(xw.T.astype(BF16), bg, preferred_element_type=F32)
            keep = jnp.where(top, elast_t[h0:h0 + 1, :], elast_t[h1:h1 + 1, :])
            h_ref[pr] = hp * keep + upd
            y_ref[:, pr * LANES:(pr + 1) * LANES] = y + dskip_ref[:, pr * LANES:(pr + 1) * LANES] * xpair

    gw = SSM_INNER // SSM_GROUPS
    for g in range(SSM_GROUPS):
        sl = slice(g * gw, (g + 1) * gw)
        u = y_ref[:, sl] * _silu(z_ref[:, sl])
        ms = jnp.sum(u * u, axis=-1, keepdims=True) * (1.0 / gw)
        u_ref[:, sl] = (u * lax.rsqrt(ms + EPS) * gout_ref[:, sl]).astype(u_ref.dtype)

    @pl.when(ck == n_ck - 1)
    def _():
        hout_ref[0] = h_ref[...]


def ssd_mixer(zxbc, dt_raw, bsz, t_len, w_conv, b_conv, dt_bias, a_log, d_skip, g_out):
    c = SSM_CHUNK
    n_ck = t_len // c
    xblk = SSM_INNER // SSM_BC
    pad = LANES - SSM_HEADS
    dtb = jnp.pad(dt_bias.astype(F32), (0, pad)).reshape(1, LANES)
    aneg = jnp.pad(-jnp.exp(a_log.astype(F32)), (0, pad)).reshape(1, LANES)
    dsk = jnp.repeat(d_skip.astype(F32), SSM_HEAD_DIM).reshape(1, SSM_INNER)
    row_spec = lambda width, blk: pl.BlockSpec((c, width), lambda i, j: (i * n_ck + j, blk))
    const = lambda shape: pl.BlockSpec(shape, lambda i, j: (0,) * len(shape))
    vmem = 2 * c * (2 * SSM_INNER + 2 * SSM_BC + LANES) * 4 + 2 * c * SSM_INNER * 2
    vmem += 3 * SSM_PAIRS * LANES * LANES * 4 + (c + 8) * SSM_XBC * 4 + c * SSM_XBC * 4 + c * SSM_INNER * 4
    vmem += 8 << 20
    u, h_out = pl.pallas_call(
        _ssd_kernel,
        grid=(bsz, n_ck),
        in_specs=[row_spec(SSM_INNER, 0), row_spec(SSM_INNER, 1),
                  row_spec(SSM_BC, 2 * xblk), row_spec(SSM_BC, 2 * xblk + 1),
                  row_spec(LANES, 0),
                  const((SSM_CONV, SSM_XBC)), const((1, SSM_XBC)), const((1, LANES)), const((1, LANES)),
                  const((1, SSM_INNER)), const((1, SSM_INNER))],
        out_specs=[pl.BlockSpec((c, SSM_INNER), lambda i, j: (i * n_ck + j, 0)),
                   pl.BlockSpec((1, SSM_PAIRS, LANES, LANES), lambda i, j: (i, 0, 0, 0))],
        out_shape=[jax.ShapeDtypeStruct((bsz * t_len, SSM_INNER), BF16),
                   jax.ShapeDtypeStruct((bsz, SSM_PAIRS, LANES, LANES), F32)],
        scratch_shapes=[pltpu.VMEM((c + 8, SSM_XBC), F32), pltpu.VMEM((c, SSM_XBC), F32),
                        pltpu.VMEM((SSM_PAIRS, LANES, LANES), F32), pltpu.VMEM((c, SSM_INNER), F32)],
        compiler_params=_cparams(("parallel", "arbitrary"), vmem),
        name="ssd_mixer",
    )(zxbc, zxbc, zxbc, zxbc, dt_raw, w_conv.astype(F32), b_conv.reshape(1, SSM_XBC).astype(F32),
      dtb, aneg, dsk, g_out.reshape(1, SSM_INNER).astype(F32))
    return u, h_out.reshape(bsz, SSM_HEADS, SSM_HEAD_DIM, SSM_STATE)


def _ssd_step_kernel(zxbc_ref, dt_ref, h_ref, conv_ref, wconv_ref, bconv_ref, dtb_ref, aneg_ref,
                     dskip_ref, gout_ref, u_ref, hout_ref, convout_ref, y_ref):
    xbc = zxbc_ref[0, :, SSM_INNER:SSM_INNER + SSM_XBC]
    prev = conv_ref[0]
    acc = bconv_ref[...] + wconv_ref[3:4, :] * xbc
    for tap in range(SSM_CONV - 1):
        acc = acc + wconv_ref[tap:tap + 1, :] * prev[tap:tap + 1, :]
    xs = _silu(acc)
    convout_ref[0, 0:2, :] = prev[1:3, :]
    convout_ref[0, 2:3, :] = xbc

    lane1 = lax.broadcasted_iota(jnp.int32, (1, LANES), 1)
    dt = jnp.where(lane1 < SSM_HEADS, _softplus(dt_ref[0] + dtb_ref[...]), 0.0)
    da = jnp.exp(dt * aneg_ref[...])
    lane = lax.broadcasted_iota(jnp.int32, (LANES, LANES), 1)
    row = lax.broadcasted_iota(jnp.int32, (LANES, LANES), 0)
    left1 = lane1 < SSM_HEAD_DIM
    top = row < SSM_HEAD_DIM
    for g in range(SSM_GROUPS):
        bg = xs[:, SSM_INNER + g * SSM_STATE:SSM_INNER + (g + 1) * SSM_STATE]
        cg = xs[:, SSM_INNER + SSM_BC + g * SSM_STATE:SSM_INNER + SSM_BC + (g + 1) * SSM_STATE]
        cb = jnp.sum(cg.astype(BF16).astype(F32) * bg.astype(BF16).astype(F32), axis=-1, keepdims=True)
        cg8 = jnp.broadcast_to(cg, (8, LANES)).astype(BF16)
        for q in range(SSM_PAIRS // SSM_GROUPS):
            pr = g * (SSM_PAIRS // SSM_GROUPS) + q
            h0, h1 = 2 * pr, 2 * pr + 1
            xpair = xs[:, pr * LANES:(pr + 1) * LANES]
            xdt = xpair * jnp.where(left1, dt[:, h0:h0 + 1], dt[:, h1:h1 + 1])
            xdt_col = jnp.broadcast_to(xdt, (LANES, LANES)).T
            hp = h_ref[0, pr]
            keep = jnp.where(top, da[:, h0:h0 + 1], da[:, h1:h1 + 1])
            hout_ref[0, pr] = hp * keep + xdt_col * bg
            y_in = lax.dot_general(cg8, hp.astype(BF16), (((1,), (1,)), ((), ())), preferred_element_type=F32)[0:1]
            y = cb * xdt + y_in * jnp.where(left1, da[:, h0:h0 + 1], da[:, h1:h1 + 1])
            y_ref[:, pr * LANES:(pr + 1) * LANES] = y + dskip_ref[:, pr * LANES:(pr + 1) * LANES] * xpair
    gw = SSM_INNER // SSM_GROUPS
    for g in range(SSM_GROUPS):
        sl = slice(g * gw, (g + 1) * gw)
        u = y_ref[:, sl] * _silu(zxbc_ref[0, :, sl])
        ms = jnp.sum(u * u, axis=-1, keepdims=True) * (1.0 / gw)
        u_ref[0, :, sl] = (u * lax.rsqrt(ms + EPS) * gout_ref[:, sl]).astype(u_ref.dtype)


def ssd_step(zxbc, dt_raw, h_prev, conv_prev, w_conv, b_conv, dt_bias, a_log, d_skip, g_out):
    bsz = zxbc.shape[0]
    pad = LANES - SSM_HEADS
    dtb = jnp.pad(dt_bias.astype(F32), (0, pad)).reshape(1, LANES)
    aneg = jnp.pad(-jnp.exp(a_log.astype(F32)), (0, pad)).reshape(1, LANES)
    dsk = jnp.repeat(d_skip.astype(F32), SSM_HEAD_DIM).reshape(1, SSM_INNER)
    const = lambda shape: pl.BlockSpec(shape, lambda i: (0,) * len(shape))
    st = (1, SSM_PAIRS, LANES, LANES)
    vmem = 4 * SSM_PAIRS * LANES * LANES * 4 + (8 << 20)
    u, h_out, conv_out = pl.pallas_call(
        _ssd_step_kernel,
        grid=(bsz,),
        in_specs=[pl.BlockSpec((1, 1, zxbc.shape[1]), lambda i: (i, 0, 0)),
                  pl.BlockSpec((1, 1, LANES), lambda i: (i, 0, 0)),
                  pl.BlockSpec(st, lambda i: (i, 0, 0, 0)),
                  pl.BlockSpec((1, SSM_CONV - 1, SSM_XBC), lambda i: (i, 0, 0)),
                  const((SSM_CONV, SSM_XBC)), const((1, SSM_XBC)), const((1, LANES)), const((1, LANES)),
                  const((1, SSM_INNER)), const((1, SSM_INNER))],
        out_specs=[pl.BlockSpec((1, 1, SSM_INNER), lambda i: (i, 0, 0)),
                   pl.BlockSpec(st, lambda i: (i, 0, 0, 0)),
                   pl.BlockSpec((1, SSM_CONV - 1, SSM_XBC), lambda i: (i, 0, 0))],
        out_shape=[jax.ShapeDtypeStruct((bsz, 1, SSM_INNER), BF16),
                   jax.ShapeDtypeStruct((bsz, SSM_PAIRS, LANES, LANES), F32),
                   jax.ShapeDtypeStruct((bsz, SSM_CONV - 1, SSM_XBC), F32)],
        scratch_shapes=[pltpu.VMEM((1, SSM_INNER), F32)],
        compiler_params=_cparams(("parallel",), vmem),
        name="ssd_step",
    )(zxbc.reshape(bsz, 1, -1), dt_raw.reshape(bsz, 1, LANES),
      h_prev.astype(F32).reshape(bsz, SSM_PAIRS, LANES, LANES), conv_prev.astype(F32),
      w_conv.astype(F32), b_conv.reshape(1, SSM_XBC).astype(F32), dtb, aneg, dsk,
      g_out.reshape(1, SSM_INNER).astype(F32))
    return u.reshape(bsz, SSM_INNER), h_out.reshape(bsz, SSM_HEADS, SSM_HEAD_DIM, SSM_STATE), conv_out


def _ffn_up_kernel(a_ref, wg_ref, wu_ref, ssq_ref, wc_ref, bc_ref, a2_ref, ssq2_ref, st2_ref,
                   o_ref, st_ref, o2_ref, st2o_ref, gp_ref, *, tiles_per_seq, tm, d_model):
    i = pl.program_id(1)

    @pl.when(i % tiles_per_seq == 0)
    def _():
        gp_ref[0:8, :] = jnp.zeros((8, gp_ref.shape[1]), F32)

    @pl.when(i == 0)
    def _():
        a2 = a2_ref[...].astype(BF16)
        r2 = _row_scale(ssq2_ref, d_model)
        gate2 = jnp.dot(a2, wg_ref[...].astype(BF16), preferred_element_type=F32) * r2
        up2 = jnp.dot(a2, wu_ref[...].astype(BF16), preferred_element_type=F32) * r2
        conv2 = bc_ref[...] + wc_ref[2:3, :] * gate2 + wc_ref[1:2, :] * st2_ref[1] + wc_ref[0:1, :] * st2_ref[0]
        o2_ref[...] = (_silu(conv2) * up2).astype(o2_ref.dtype)
        st2o_ref[0] = st2_ref[1]
        st2o_ref[1] = gate2

    a = a_ref[...]
    r = _row_scale(ssq_ref, d_model)
    gate = jnp.dot(a, wg_ref[...].astype(BF16), preferred_element_type=F32) * r
    up = jnp.dot(a, wu_ref[...].astype(BF16), preferred_element_type=F32) * r
    gp_ref[8:8 + tm, :] = gate
    conv = bc_ref[...] + wc_ref[2:3, :] * gate
    conv = conv + wc_ref[1:2, :] * gp_ref[7:7 + tm, :]
    conv = conv + wc_ref[0:1, :] * gp_ref[6:6 + tm, :]
    o_ref[...] = (_silu(conv) * up).astype(o_ref.dtype)
    st_ref[0] = gp_ref[tm + 6:tm + 8, :]
    gp_ref[0:8, :] = gp_ref[tm:tm + 8, :]


def ffn_up(xg, row_ssq, bsz, t_len, xg2, row_ssq2, state2, w_up, w_conv, b_conv, layer, tm=1024, tn=256):
    m, d = xg.shape
    b2 = xg2.shape[0]
    d_ff = w_up.shape[2] // 2
    tm = min(tm, t_len)
    nj = d_ff // tn
    assert d_ff % tn == 0 and t_len % tm == 0 and tm % 8 == 0
    tps = t_len // tm
    vmem = 2 * tm * d * 2 + 4 * d * tn * 4 + 2 * d * tn * 2 + 2 * tm * tn * 2 + (tm + 8) * tn * 4 + 6 * tm * tn * 4
    kern = functools.partial(_ffn_up_kernel, tiles_per_seq=tps, tm=tm, d_model=d)
    return pl.pallas_call(
        kern,
        grid=(nj, m // tm),
        in_specs=[pl.BlockSpec((tm, d), lambda j, i: (i, 0)),
                  pl.BlockSpec((None, d, tn), lambda j, i: (layer, 0, j)),
                  pl.BlockSpec((None, d, tn), lambda j, i: (layer, 0, j + nj)),
                  pl.BlockSpec((tm, LANES), lambda j, i: (i, 0)),
                  pl.BlockSpec((None, FFN_CONV, tn), lambda j, i: (layer, 0, j)),
                  pl.BlockSpec((None, 1, tn), lambda j, i: (layer, 0, j)),
                  pl.BlockSpec((b2, d), lambda j, i: (0, 0)),
                  pl.BlockSpec((b2, LANES), lambda j, i: (0, 0)),
                  pl.BlockSpec((FFN_CONV - 1, b2, tn), lambda j, i: (0, 0, j))],
        out_specs=[pl.BlockSpec((tm, tn), lambda j, i: (i, j)),
                   pl.BlockSpec((1, FFN_CONV - 1, tn), lambda j, i: (i // tps, 0, j)),
                   pl.BlockSpec((b2, tn), lambda j, i: (0, j)),
                   pl.BlockSpec((FFN_CONV - 1, b2, tn), lambda j, i: (0, 0, j))],
        out_shape=[jax.ShapeDtypeStruct((m, d_ff), BF16),
                   jax.ShapeDtypeStruct((bsz, FFN_CONV - 1, d_ff), F32),
                   jax.ShapeDtypeStruct((b2, d_ff), F32),
                   jax.ShapeDtypeStruct((FFN_CONV - 1, b2, d_ff), F32)],
        scratch_shapes=[pltpu.VMEM((tm + 8, tn), F32)],
        compiler_params=_cparams(("parallel", "arbitrary"), vmem),
        name="ffn_up",
    )(xg, w_up, w_up, row_ssq, w_conv.astype(F32), b_conv.reshape(-1, 1, d_ff).astype(F32),
      xg2, row_ssq2, state2)


def _dil_attn_kernel(q0_ref, q1_ref, q2_ref, k0_ref, k1_ref, k2_ref, v0_ref, v1_ref, v2_ref,
                     o_ref, m_ref, l_ref, acc_ref, *, t_len):
    q_refs, k_refs, v_refs = (q0_ref, q1_ref, q2_ref), (k0_ref, k1_ref, k2_ref), (v0_ref, v1_ref, v2_ref)
    scale = HEAD_DIM ** -0.5
    blk = 128
    ii = lax.broadcasted_iota(jnp.int32, (blk, blk), 0)
    jj = lax.broadcasted_iota(jnp.int32, (blk, blk), 1)
    cur_ok = jj <= ii
    prev_ok = jj >= ii
    nt = (((1,), (1,)), ((), ()))
    for g, (w, r) in enumerate(DIL_PATTERNS):
        assert w // r == blk
        n_blk = t_len // (r * blk)
        for cls in range(r):
            for qb in range(n_blk):
                rows = pl.ds(cls + qb * blk * r, blk, stride=r) if r > 1 else pl.ds(qb * blk, blk)
                q = q_refs[g][rows, :].astype(BF16)
                k = k_refs[g][rows, :].astype(BF16)
                v = v_refs[g][rows, :].astype(BF16)
                s = lax.dot_general(q, k, nt, preferred_element_type=F32) * scale
                s = jnp.where(cur_ok, s, -jnp.inf)
                m = jnp.max(s, axis=-1, keepdims=True)
                if qb > 0:
                    prow = (pl.ds(cls + (qb - 1) * blk * r, blk, stride=r) if r > 1
                            else pl.ds((qb - 1) * blk, blk))
                    kp = k_refs[g][prow, :].astype(BF16)
                    vp = v_refs[g][prow, :].astype(BF16)
                    sp = lax.dot_general(q, kp, nt, preferred_element_type=F32) * scale
                    sp = jnp.where(prev_ok, sp, -jnp.inf)
                    m = jnp.maximum(m, jnp.max(sp, axis=-1, keepdims=True))
                p = jnp.exp(s - m)
                den = jnp.sum(p, axis=-1, keepdims=True)
                acc = jnp.dot(p.astype(BF16), v, preferred_element_type=F32)
                if qb > 0:
                    pp = jnp.exp(sp - m)
                    den = den + jnp.sum(pp, axis=-1, keepdims=True)
                    acc = acc + jnp.dot(pp.astype(BF16), vp, preferred_element_type=F32)
                m = jnp.broadcast_to(m, (blk, HEAD_DIM))
                den = jnp.broadcast_to(den, (blk, HEAD_DIM))
                if g == 0:
                    m_ref[rows, :] = m
                    l_ref[rows, :] = den
                    acc_ref[rows, :] = acc
                else:
                    m_old = m_ref[rows, :]
                    m_new = jnp.maximum(m_old, m)
                    a_old = jnp.exp(m_old - m_new)
                    a_new = jnp.exp(m - m_new)
                    m_ref[rows, :] = m_new
                    l_ref[rows, :] = l_ref[rows, :] * a_old + den * a_new
                    acc_ref[rows, :] = acc_ref[rows, :] * a_old + acc * a_new
    o_ref[...] = (acc_ref[...] / l_ref[...]).astype(o_ref.dtype)


def dilated_attention_prompt(qq, kv, bsz, t_len):
    nh = DIL_HEADS
    tspec = lambda colfn: pl.BlockSpec((t_len, HEAD_DIM), lambda b, h: (b, colfn(h)))
    in_specs = ([tspec(lambda h, g=g: g * nh + h) for g in range(3)]
                + [tspec(lambda h, g=g: g * 2 * nh + h) for g in range(3)]
                + [tspec(lambda h, g=g: g * 2 * nh + nh + h) for g in range(3)])
    vmem = 2 * 9 * t_len * HEAD_DIM * 4 + 2 * t_len * HEAD_DIM * 2 + 3 * t_len * HEAD_DIM * 4 + (8 << 20)
    return pl.pallas_call(
        functools.partial(_dil_attn_kernel, t_len=t_len),
        grid=(bsz, nh),
        in_specs=in_specs,
        out_specs=pl.BlockSpec((t_len, HEAD_DIM), lambda b, h: (b, h)),
        out_shape=jax.ShapeDtypeStruct((bsz * t_len, nh * HEAD_DIM), BF16),
        scratch_shapes=[pltpu.VMEM((t_len, HEAD_DIM), F32)] * 3,
        compiler_params=_cparams(("parallel", "parallel"), vmem),
        name="dilated_attention",
    )(qq, qq, qq, kv, kv, kv, kv, kv, kv)


def _dil_step_kernel(q_ref, kvn_ref, c0_ref, c1_ref, c2_ref, o_ref):
    caches = (c0_ref, c1_ref, c2_ref)
    scale = HEAD_DIM ** -0.5
    nh = DIL_HEADS
    s_past, s_new = [], []
    m = None
    for g in range(3):
        q = q_ref[0, g]
        s = jnp.sum(caches[g][0, :, 0:nh, :] * q[None], axis=-1, keepdims=True) * scale
        sn = jnp.sum(kvn_ref[0, g, 0] * q, axis=-1, keepdims=True) * scale
        s_past.append(s)
        s_new.append(sn)
        mg = jnp.maximum(jnp.max(s, axis=0), sn)
        m = mg if m is None else jnp.maximum(m, mg)
    den = jnp.zeros((nh, 1), F32)
    acc = jnp.zeros((nh, HEAD_DIM), F32)
    for g in range(3):
        p = jnp.exp(s_past[g] - m[None])
        pn = jnp.exp(s_new[g] - m)
        den = den + jnp.sum(p, axis=0) + pn
        acc = acc + jnp.sum(p * caches[g][0, :, nh:2 * nh, :], axis=0) + pn * kvn_ref[0, g, 1]
    o_ref[0] = (acc / den).astype(o_ref.dtype)


def dilated_attention_step(q, kv_new, caches):
    bsz = q.shape[0]
    nh = DIL_HEADS
    views = []
    for cache, (w, r) in zip(caches, DIL_PATTERNS):
        assert cache.shape[1] == w, "rolling window cache must hold the full window"
        views.append(cache.reshape(bsz, w // r, r * 2 * nh, HEAD_DIM))
    nkeys = DIL_PATTERNS[0][0] // DIL_PATTERNS[0][1]
    return pl.pallas_call(
        _dil_step_kernel,
        grid=(bsz,),
        in_specs=[pl.BlockSpec((1, 3, nh, HEAD_DIM), lambda b: (b, 0, 0, 0)),
                  pl.BlockSpec((1, 3, 2, nh, HEAD_DIM), lambda b: (b, 0, 0, 0, 0))]
                 + [pl.BlockSpec((1, nkeys, 2 * nh, HEAD_DIM), lambda b: (b, 0, 0, 0)) for _ in DIL_PATTERNS],
        out_specs=pl.BlockSpec((1, nh, HEAD_DIM), lambda b: (b, 0, 0)),
        out_shape=jax.ShapeDtypeStruct((bsz, nh, HEAD_DIM), F32),
        compiler_params=_cparams(("parallel",), 2 * 3 * nkeys * 2 * nh * HEAD_DIM * 4 + (8 << 20)),
        name="dilated_attention_step",
    )(q, kv_new, *views)


def _mem_step_kernel(q_ref, kv_ref, o_ref):
    q = q_ref[0]
    s = jnp.sum(kv_ref[0, :, 0] * q[None], axis=-1, keepdims=True) * (MEM_HEAD_DIM ** -0.5)
    p = jnp.exp(s - jnp.max(s, axis=0)[None])
    o_ref[0] = (jnp.sum(p * kv_ref[0, :, 1], axis=0) / jnp.sum(p, axis=0)).astype(o_ref.dtype)


def mem_attention_step(q, cache, layer):
    bsz = q.shape[0]
    return pl.pallas_call(
        _mem_step_kernel,
        grid=(bsz,),
        in_specs=[pl.BlockSpec((1, MEM_HEADS, MEM_HEAD_DIM), lambda b: (b, 0, 0)),
                  pl.BlockSpec((None, 1, N_MEM, 2, MEM_HEADS, MEM_HEAD_DIM), lambda b: (layer, b, 0, 0, 0, 0))],
        out_specs=pl.BlockSpec((1, MEM_HEADS, MEM_HEAD_DIM), lambda b: (b, 0, 0)),
        out_shape=jax.ShapeDtypeStruct((bsz, MEM_HEADS, MEM_HEAD_DIM), F32),
        compiler_params=_cparams(("parallel",), 4 * N_MEM * 2 * 8 * MEM_HEAD_DIM * 4 + (8 << 20)),
        name="mem_attention_step",
    )(q, cache)


class _Group:
    def __init__(self, x, bsz, t_len, pos, mem_kv, ssm_prev=None, conv_prev=None, win_past=None):
        self.x, self.bsz, self.t_len, self.mem_kv = x, bsz, t_len, mem_kv
        self.ssm_prev, self.conv_prev, self.win_past = ssm_prev, conv_prev, win_past
        self.prompt = ssm_prev is None
        m = x.shape[0]
        self.rope = (rope_tables(pos) if self.prompt
                     else tuple(jnp.broadcast_to(t, (m, HEAD_DIM)) for t in rope_tables(pos)))

    def memory_attention(self, q, q_colblk, layer):
        m = q.shape[0]
        if self.prompt:
            return mem_attention(q.reshape(self.bsz, self.t_len, -1), q_colblk, self.mem_kv[layer]).reshape(m, MEM_W)
        q = q[:, q_colblk * MEM_W:(q_colblk + 1) * MEM_W].reshape(m, MEM_HEADS, MEM_HEAD_DIM)
        return mem_attention_step(q, self.mem_kv, layer).reshape(m, MEM_W)


def _mixer_a(gp, gs, p):
    d = gp.x.shape[1]
    wt_in = jnp.swapaxes(p["w_in_a"], 1, 2)
    zx_cols = SSM_INNER + SSM_XBC
    hp, hs = rmsnorm_rows(gp.x, p["g_mix"][0]), rmsnorm_rows(gs.x, p["g_mix"][0])
    ride = dict(a_list=[hs])
    zxbc_p, zxbc_s = matmul([hp], wt_in, 0, n=zx_cols, w_t=True, rider=ride, name="in_proj_a")
    dt_p, dt_s = matmul([hp], wt_in, 0, n=LANES, w_col0=zx_cols, tn=LANES, w_t=True, rider=ride, name="in_proj_dt")
    qm_p, qm_s = matmul([hp], wt_in[:, zx_cols + SSM_HEADS:], 0, n=MEM_W, w_t=True, modes=["norm256"] * 2,
                        gain=jnp.tile(p["g_mem_q"][0], MEM_HEADS), rider=ride, name="in_proj_qmem")
    ymem_p, ymem_s = gp.memory_attention(qm_p, 0, 0), gs.memory_attention(qm_s, 0, 0)
    ssm_args = (p["w_conv_a"][0], p["b_conv_a"][0], p["dt_bias_a"][0], p["a_log_a"][0], p["d_skip_a"][0],
                p["g_ssm_out_a"][0])
    u_p, gp.ssm_new = ssd_mixer(zxbc_p, dt_p, gp.bsz, gp.t_len, *ssm_args)
    gp.conv_new = zxbc_p.reshape(gp.bsz, gp.t_len, zx_cols)[:, gp.t_len - (SSM_CONV - 1):, SSM_INNER:]
    u_s, gs.ssm_new, gs.conv_new = ssd_step(zxbc_s, dt_s, gs.ssm_prev[0], gs.conv_prev[0], *ssm_args)
    out_p, out_s = matmul([u_p, ymem_p], p["w_out_a"], 0, n=d, res=gp.x, emit_gains=[p["g_ffn"][0]],
                          rider=dict(a_list=[u_s, ymem_s], res=gs.x), name="out_proj_a")
    (gp.x, gp.xg, gp.ssq), (gs.x, gs.xg, gs.ssq) = out_p, out_s


def _mixer_b(gp, gs, xg_kv, xg_mix, p):
    d = gp.x.shape[1]
    ms = gs.x.shape[0]
    nh = DIL_HEADS
    gk = jnp.concatenate([jnp.concatenate([jnp.tile(p["g_k_dil"][i], nh), jnp.ones((nh * HEAD_DIM,), F32)])
                          for i in range(3)])
    gp.kv, gs.kv = matmul([xg_kv[0]], p["w_kv"][None], 0, n=6 * nh * HEAD_DIM, row_ssq=(gp.ssq, d),
                          modes=(["norm128rope"] * 2 + ["plain"] * 2) * 3, gain=gk, rope=gp.rope,
                          rider=dict(a_list=[xg_kv[1]], row_ssq=gs.ssq, rope=gs.rope), name="kv_proj")
    gq = jnp.concatenate([jnp.tile(p["g_q_dil"][0][i], nh) for i in range(3)]
                         + [jnp.tile(p["g_mem_q"][1], MEM_HEADS)])
    qq_p, qq_s = matmul([xg_mix[0]], p["w_in_b"], 0, n=d, row_ssq=(gp.ssq, d),
                        modes=["norm128rope"] * 6 + ["norm256"] * 2, gain=gq, rope=gp.rope,
                        rider=dict(a_list=[xg_mix[1]], row_ssq=gs.ssq, rope=gs.rope), name="in_proj_b")
    ymem_p, ymem_s = gp.memory_attention(qq_p, 3, 1), gs.memory_attention(qq_s, 3, 1)
    att_p = dilated_attention_prompt(qq_p, gp.kv, gp.bsz, gp.t_len)
    att_s = dilated_attention_step(qq_s[:, :3 * nh * HEAD_DIM].reshape(ms, 3, nh, HEAD_DIM),
                                   gs.kv.reshape(ms, 3, 2, nh, HEAD_DIM), gs.win_past).reshape(ms, nh * HEAD_DIM)
    out_p, out_s = matmul([att_p, ymem_p], p["w_out_b"], 0, n=d, res=gp.x, emit_gains=[p["g_ffn"][1]],
                          rider=dict(a_list=[att_s, ymem_s], res=gs.x), name="out_proj_b")
    (gp.x, gp.xg, gp.ssq), (gs.x, gs.xg, gs.ssq) = out_p, out_s


def _conv_ffn(gp, gs, ffn_prev, layer, p, emit_gains=()):
    state_s = jnp.swapaxes(ffn_prev[layer].astype(F32), 0, 1)
    a_p, st_p, a_s, st_s = ffn_up(gp.xg, gp.ssq, gp.bsz, gp.t_len, gs.xg, gs.ssq, state_s,
                                  p["w_ffn_up"], p["w_ffn_conv"], p["b_ffn_conv"], layer)
    out_p, out_s = matmul_long_k(a_p, p["w_ffn_down"], layer, gp.x, a_s, gs.x, emit_gains=emit_gains,
                                 name="ffn_down")
    return out_p, out_s, st_p, jnp.swapaxes(st_s, 0, 1)


def _trunk(gp, gs, ffn_prev, p):
    _mixer_a(gp, gs, p)
    out_p, out_s, st0_p, st0_s = _conv_ffn(gp, gs, ffn_prev, 0, p, emit_gains=[p["g_kv"], p["g_mix"][1]])
    (gp.x, kv_p, mix_p, gp.ssq), (gs.x, kv_s, mix_s, gs.ssq) = out_p, out_s
    _mixer_b(gp, gs, (kv_p, kv_s), (mix_p, mix_s), p)
    out_p, out_s, st1_p, st1_s = _conv_ffn(gp, gs, ffn_prev, 1, p)
    return out_p[0], out_s[0], jnp.stack([st0_p, st1_p], axis=0), jnp.stack([st0_s, st1_s], axis=0)


def _memory_kv(mem, g_norm, w_kv, layer, g_k):
    bsz, n, d = mem.shape
    hm = rmsnorm_rows(mem.reshape(bsz * n, d), g_norm)
    gain = jnp.concatenate([jnp.tile(g_k, MEM_HEADS), jnp.ones((MEM_W,), F32)])
    kv = matmul([hm], w_kv, layer, n=2 * MEM_W, modes=["norm256"] * 2 + ["plain"] * 2, gain=gain,
                name="mem_kv_proj")
    return kv.reshape(bsz, n, 2 * MEM_W)


def kernel(x_prompt, x_sample, state_ssm, state_ssm_conv, state_ffn_conv, cache_mem_kv, cache_win_kv0, cache_win_kv1, cache_win_kv2, mem_prompt, g_mix, w_in_a, w_conv_a, b_conv_a, dt_bias_a, a_log_a, d_skip_a, g_ssm_out_a, w_out_a, g_kv, w_kv, g_k_dil, w_in_b, g_q_dil, w_out_b, g_mem, w_mem_kv, g_mem_q, g_mem_k, g_ffn, w_ffn_up, w_ffn_conv, b_ffn_conv, w_ffn_down):
    p = dict(g_mix=g_mix, w_in_a=w_in_a, w_conv_a=w_conv_a, b_conv_a=b_conv_a, dt_bias_a=dt_bias_a,
             a_log_a=a_log_a, d_skip_a=d_skip_a, g_ssm_out_a=g_ssm_out_a, w_out_a=w_out_a, g_kv=g_kv, w_kv=w_kv,
             g_k_dil=g_k_dil, w_in_b=w_in_b, g_q_dil=g_q_dil, w_out_b=w_out_b, g_mem_q=g_mem_q, g_ffn=g_ffn,
             w_ffn_up=w_ffn_up, w_ffn_conv=w_ffn_conv, b_ffn_conv=b_ffn_conv, w_ffn_down=w_ffn_down)
    bp, sp, d = x_prompt.shape
    bs, ds, _ = x_sample.shape
    depth = g_mix.shape[0]
    nh = DIL_HEADS

    mem_kv_p = [_memory_kv(mem_prompt, g_mem[i], w_mem_kv, i, g_mem_k[i]) for i in range(depth)]
    gp = _Group(x_prompt.reshape(bp * sp, d), bp, sp, jnp.arange(sp, dtype=jnp.int32), mem_kv_p)
    assert ds == 1
    gs = _Group(x_sample.reshape(bs * ds, d), bs, ds, PAST_LEN + jnp.arange(ds, dtype=jnp.int32), cache_mem_kv,
                state_ssm, state_ssm_conv, [cache_win_kv0, cache_win_kv1, cache_win_kv2])
    y_p, y_s, ffn_p, ffn_s = _trunk(gp, gs, state_ffn_conv, p)

    kv_p = gp.kv.reshape(bp, sp, 3, 2, nh, HEAD_DIM)
    kv_s = gs.kv.reshape(bs, ds, 3, 2, nh, HEAD_DIM)
    win_p = [kv_p[:, sp - min(w, sp):, g] for g, (w, _) in enumerate(DIL_PATTERNS)]
    mem_out = jnp.stack(mem_kv_p, axis=0).reshape(depth, bp, N_MEM, 2, MEM_HEADS, MEM_HEAD_DIM)
    return (y_p.reshape(bp, sp, d), y_s.reshape(bs, ds, d), gp.ssm_new[None], gs.ssm_new[None],
            gp.conv_new[None], gs.conv_new[None], ffn_p, ffn_s, mem_out, win_p[0], win_p[1], win_p[2],
            kv_s[:, :, 0], kv_s[:, :, 1], kv_s[:, :, 2])
```
